```python
import math
import jax, jax.numpy as jnp
from jax import lax
import numpy as np

D_MODEL = 1024
BATCH = 4
SEQ = 4096
DEPTH = 1

CHUNK = 64
Q_BLOCK = 128
N_HEADS = 8
HEAD_DIM = 64
V_DIM = 2 * HEAD_DIM
QK_WIDTH = N_HEADS * 2 * HEAD_DIM
ATTN_WIDTH = N_HEADS * V_DIM
CONV_CH = D_MODEL
CONV_WIDTH = 31
D_FF = 4 * D_MODEL
ROPE_THETA = 10000.0
LN_EPS = 1e-5
DEEPNORM_ALPHA = (2.0 * DEPTH) ** 0.25
DEEPNORM_BETA = (8.0 * DEPTH) ** -0.25

SPLIT_SIZES = (QK_WIDTH, QK_WIDTH, ATTN_WIDTH, 2 * CONV_CH, 2 * D_MODEL)
IN_WIDTH = sum(SPLIT_SIZES)
SPLIT_POINTS = tuple(int(v) for v in np.cumsum(SPLIT_SIZES)[:-1])

kernel_name = "hybrid_diffattn_conformer_conv_gated"


def lambda_init_for(layer_idx):
    return 0.8 - 0.6 * math.exp(-0.3 * layer_idx)


def layer_norm(x, g, b):
    xf = x.astype(jnp.float32)
    mu = jnp.mean(xf, axis=-1, keepdims=True)
    var = jnp.mean(jnp.square(xf - mu), axis=-1, keepdims=True)
    y = (xf - mu) * lax.rsqrt(var + LN_EPS) * g.astype(jnp.float32) + b.astype(jnp.float32)
    return y.astype(x.dtype)


def rms_norm(x, g):
    xf = x.astype(jnp.float32)
    y = xf * lax.rsqrt(jnp.mean(jnp.square(xf), axis=-1, keepdims=True) + LN_EPS)
    return (y * g.astype(jnp.float32)).astype(x.dtype)


def apply_rope(t, cos, sin):
    half = HEAD_DIM // 2
    t1, t2 = t[..., :half], t[..., half:]
    return jnp.concatenate([t1 * cos - t2 * sin, t1 * sin + t2 * cos], axis=-1)


def chunk_causal_diff_attention(q, k, v, lam):
    seq = q.shape[3]
    scale = HEAD_DIM ** -0.5
    neg = jnp.finfo(jnp.float32).min
    outs = []
    for i in range(seq // Q_BLOCK):
        q0 = i * Q_BLOCK
        kend = q0 + Q_BLOCK
        qb = q[:, :, :, q0:kend]
        kb = k[:, :, :, :kend]
        s = jnp.einsum('bhcqd,bhckd->bhcqk', qb, kb).astype(jnp.float32) * scale
        q_chunk = (q0 + jnp.arange(Q_BLOCK)) // CHUNK
        k_chunk = jnp.arange(kend) // CHUNK
        mask = k_chunk[None, :] <= q_chunk[:, None]
        s = jnp.where(mask, s, neg)
        p = jax.nn.softmax(s, axis=-1)
        a = p[:, :, 0] - lam * p[:, :, 1]
        outs.append(jnp.einsum('bhqk,bhkv->bhqv', a.astype(v.dtype), v[:, :, :kend]))
    return jnp.concatenate(outs, axis=2)


def causal_depthwise_conv(u, kernel, bias):
    out = lax.conv_general_dilated(
        u, kernel[:, None, :].astype(u.dtype), window_strides=(1,),
        padding=((CONV_WIDTH - 1, 0),),
        dimension_numbers=('NWC', 'WIO', 'NWC'),
        feature_group_count=u.shape[-1])
    return out + bias


def setup_inputs(seed: int = 0) -> dict:
    key = jax.random.key(seed)
    ks = jax.random.split(key, 32)
    L, D = DEPTH, D_MODEL
    f32 = jnp.float32

    def nrm(k, shape, scale):
        return jax.random.normal(k, shape, f32) * scale

    x = jax.random.normal(ks[0], (BATCH, SEQ, D), f32)
    offset = jax.random.randint(ks[1], (BATCH, 1), 0, 1024, dtype=jnp.int32)
    positions = (offset + jnp.arange(SEQ, dtype=jnp.int32)[None, :]).astype(jnp.int32)

    s_in = D ** -0.5
    w_q = nrm(ks[2], (L, D, QK_WIDTH), s_in)
    w_k = nrm(ks[3], (L, D, QK_WIDTH), s_in)
    w_v = nrm(ks[4], (L, D, ATTN_WIDTH), s_in * DEEPNORM_BETA)
    w_glu = nrm(ks[5], (L, D, 2 * CONV_CH), s_in)
    w_gate = nrm(ks[6], (L, D, 2 * D), s_in)
    w_in = jnp.concatenate([w_q, w_k, w_v, w_glu, w_gate], axis=-1)

    return {
        "x": x,
        "positions": positions,
        "w_in": w_in,
        "b_glu": nrm(ks[7], (L, 2 * CONV_CH), 0.02),
        "b_gate": nrm(ks[8], (L, 2 * D), 0.02),
        "lambda_q1": nrm(ks[9], (L, HEAD_DIM), 0.1),
        "lambda_k1": nrm(ks[10], (L, HEAD_DIM), 0.1),
        "lambda_q2": nrm(ks[11], (L, HEAD_DIM), 0.1),
        "lambda_k2": nrm(ks[12], (L, HEAD_DIM), 0.1),
        "subln_g": 1.0 + nrm(ks[13], (L, V_DIM), 0.02),
        "dw_kernel": nrm(ks[14], (L, CONV_WIDTH, CONV_CH), CONV_WIDTH ** -0.5),
        "dw_bias": nrm(ks[15], (L, CONV_CH), 0.02),
        "conv_ln_g": 1.0 + nrm(ks[16], (L, CONV_CH), 0.02),
        "conv_ln_b": nrm(ks[17], (L, CONV_CH), 0.02),
        "w_pw2": nrm(ks[18], (L, CONV_CH, D), CONV_CH ** -0.5 * DEEPNORM_BETA),
        "b_pw2": nrm(ks[19], (L, D), 0.02),
        "w_out": nrm(ks[20], (L, D, D), D ** -0.5 * DEEPNORM_BETA),
        "ln1_g": 1.0 + nrm(ks[21], (L, D), 0.02),
        "ln1_b": nrm(ks[22], (L, D), 0.02),
        "w_ff1": nrm(ks[23], (L, D, D_FF), D ** -0.5),
        "w_ff2": nrm(ks[24], (L, D_FF, D), D_FF ** -0.5 * DEEPNORM_BETA),
        "ln2_g": 1.0 + nrm(ks[25], (L, D), 0.02),
        "ln2_b": nrm(ks[26], (L, D), 0.02),
    }


def reference(x, positions, w_in, b_glu, b_gate, lambda_q1, lambda_k1, lambda_q2,
              lambda_k2, subln_g, dw_kernel, dw_bias, conv_ln_g, conv_ln_b, w_pw2,
              b_pw2, w_out, ln1_g, ln1_b, w_ff1, w_ff2, ln2_g, ln2_b):
    B, S, _ = x.shape
    half = HEAD_DIM // 2
    inv_freq = ROPE_THETA ** (-jnp.arange(half, dtype=jnp.float32) * 2.0 / HEAD_DIM)
    ang = positions.astype(jnp.float32)[..., None] * inv_freq
    cos = jnp.cos(ang)[:, :, None, None, :].astype(x.dtype)
    sin = jnp.sin(ang)[:, :, None, None, :].astype(x.dtype)

    h = x
    for l in range(DEPTH):
        lam_init = lambda_init_for(l)
        proj = jnp.einsum('bsd,de->bse', h, w_in[l])
        q, k, v, glu, gates = jnp.split(proj, SPLIT_POINTS, axis=-1)

        q = apply_rope(q.reshape(B, S, N_HEADS, 2, HEAD_DIM), cos, sin)
        k = apply_rope(k.reshape(B, S, N_HEADS, 2, HEAD_DIM), cos, sin)
        q = jnp.transpose(q, (0, 2, 3, 1, 4))
        k = jnp.transpose(k, (0, 2, 3, 1, 4))
        v = jnp.transpose(v.reshape(B, S, N_HEADS, V_DIM), (0, 2, 1, 3))
        lam = (jnp.exp(jnp.sum(lambda_q1[l].astype(jnp.float32) * lambda_k1[l].astype(jnp.float32)))
               - jnp.exp(jnp.sum(lambda_q2[l].astype(jnp.float32) * lambda_k2[l].astype(jnp.float32)))
               + lam_init)
        att = chunk_causal_diff_attention(q, k, v, lam)
        att = rms_norm(att, subln_g[l]) * (1.0 - lam_init)
        att = jnp.transpose(att, (0, 2, 1, 3)).reshape(B, S, ATTN_WIDTH)

        glu = glu + b_glu[l]
        ga, gb = jnp.split(glu, 2, axis=-1)
        u = ga * jax.nn.sigmoid(gb)
        u = causal_depthwise_conv(u, dw_kernel[l], dw_bias[l])
        u = jax.nn.silu(layer_norm(u, conv_ln_g[l], conv_ln_b[l]))
        conv = jnp.einsum('bsc,cd->bsd', u, w_pw2[l]) + b_pw2[l]

        g = jax.nn.sigmoid(gates + b_gate[l])
        g_att, g_conv = jnp.split(g, 2, axis=-1)
        mixed = jnp.einsum('bsd,de->bse', g_att * att + g_conv * conv, w_out[l])
        h = layer_norm(DEEPNORM_ALPHA * h + mixed, ln1_g[l], ln1_b[l])

        ff = jnp.square(jax.nn.relu(jnp.einsum('bsd,df->bsf', h, w_ff1[l])))
        ff = jnp.einsum('bsf,fd->bsd', ff, w_ff2[l])
        h = layer_norm(DEEPNORM_ALPHA * h + ff, ln2_g[l], ln2_b[l])
    return h
```

```python
import functools
import math

import jax
import jax.numpy as jnp
from jax import lax
from jax.experimental import pallas as pl
from jax.experimental.pallas import tpu as pltpu

D_MODEL = 1024
N_HEADS = 8
HEAD_DIM = 64
V_DIM = 2 * HEAD_DIM
CHUNK = 64
CONV_WIDTH = 31
D_FF = 4 * D_MODEL
ROPE_THETA = 10000.0
LN_EPS = 1e-5
DEPTH = 1
DEEPNORM_ALPHA = (2.0 * DEPTH) ** 0.25
LAMBDA_INIT = 0.8 - 0.6 * math.exp(-0.3 * 0)

LANES = 128
SUBLANES = 8
VMEM_LIMIT = 56 * 1024 * 1024

ROW_TILE = 512
Q_TILE = 256
K_TILE = 256
CONV_HALO = 32
NEG_BIG = -1e30


def _layer_norm(y, g, b):
    mu = jnp.mean(y, axis=-1, keepdims=True)
    d = y - mu
    var = jnp.mean(d * d, axis=-1, keepdims=True)
    return d * lax.rsqrt(var + LN_EPS) * g + b


def _qkv_kernel(x_ref, pos_ref, inv_ref, w_ref, q_ref, k_ref, v_ref):
    xb = x_ref[...].astype(jnp.bfloat16)
    proj = jnp.dot(xb, w_ref[...], preferred_element_type=jnp.float32)
    ang = pos_ref[...] * inv_ref[...]
    lane = lax.broadcasted_iota(jnp.int32, (1, LANES), 1)
    first_half = (lane % HEAD_DIM) < (HEAD_DIM // 2)
    cos = jnp.cos(ang)
    sin = jnp.where(first_half, -jnp.sin(ang), jnp.sin(ang))
    for idx, out_ref, scale in ((0, q_ref, HEAD_DIM ** -0.5), (1, k_ref, 1.0)):
        for h in range(N_HEADS):
            c0 = idx * D_MODEL + h * LANES
            t = proj[:, c0:c0 + LANES]
            swapped = jnp.where(first_half,
                                pltpu.roll(t, LANES - HEAD_DIM // 2, 1),
                                pltpu.roll(t, HEAD_DIM // 2, 1))
            r = t * cos + swapped * sin
            if scale != 1.0:
                r = r * scale
            out_ref[:, h * LANES:(h + 1) * LANES] = r.astype(out_ref.dtype)
    v_ref[...] = proj[:, 2 * D_MODEL:].astype(v_ref.dtype)


def _qkv_call(x2, pos2, inv_lane, w_qkv):
    T = x2.shape[0]
    tm = ROW_TILE
    out = jax.ShapeDtypeStruct((T, D_MODEL), jnp.bfloat16)
    row = lambda i: (i, 0)
    const = lambda i: (0, 0)
    return pl.pallas_call(
        _qkv_kernel,
        grid=(T // tm,),
        in_specs=[pl.BlockSpec((tm, D_MODEL), row),
                  pl.BlockSpec((tm, 1), row),
                  pl.BlockSpec((1, LANES), const),
                  pl.BlockSpec((D_MODEL, 3 * D_MODEL), const)],
        out_specs=[pl.BlockSpec((tm, D_MODEL), row)] * 3,
        out_shape=[out, out, out],
        compiler_params=pltpu.CompilerParams(
            dimension_semantics=("arbitrary",), vmem_limit_bytes=VMEM_LIMIT),
        name="qkv_rope",
    )(x2, pos2, inv_lane, w_qkv)


def _attn_kernel(lam_ref, g_ref, q_ref, k_ref, v_ref, o_ref):
    S = q_ref.shape[1]
    nq = S // Q_TILE
    lp = lam_ref[...]
    lam = (jnp.exp(jnp.sum(lp[0:1] * lp[1:2], axis=-1, keepdims=True))
           - jnp.exp(jnp.sum(lp[2:3] * lp[3:4], axis=-1, keepdims=True))
           + LAMBDA_INIT)
    lane = lax.broadcasted_iota(jnp.int32, (1, LANES), 1)
    comp0 = lane < HEAD_DIM
    row_chunk = lax.broadcasted_iota(jnp.int32, (Q_TILE, K_TILE), 0) // CHUNK
    col_chunk = lax.broadcasted_iota(jnp.int32, (Q_TILE, K_TILE), 1) // CHUNK
    diag_mask = col_chunk <= row_chunk

    def q_body(qi, carry):
        q0 = pl.multiple_of(qi * Q_TILE, Q_TILE)
        q = q_ref[0, pl.ds(q0, Q_TILE), :]
        zero = jnp.zeros_like(q)
        qc = (jnp.where(comp0, q, zero), jnp.where(comp0, zero, q))

        def step(kj, state, masked):
            k0 = pl.multiple_of(kj * K_TILE, K_TILE)
            kt = k_ref[0, pl.ds(k0, K_TILE), :]
            vt = v_ref[0, pl.ds(k0, K_TILE), :]
            new = []
            for c in range(2):
                m, l, acc = state[c]
                s = lax.dot_general(qc[c], kt, (((1,), (1,)), ((), ())),
                                    preferred_element_type=jnp.float32)
                if masked:
                    s = jnp.where(diag_mask, s, NEG_BIG)
                m_new = jnp.maximum(m, jnp.max(s, axis=-1, keepdims=True))
                p = jnp.exp(s - m_new)
                alpha = jnp.exp(m - m_new)
                l_new = alpha * l + jnp.sum(p, axis=-1, keepdims=True)
                acc_new = alpha * acc + jnp.dot(p.astype(jnp.bfloat16), vt,
                                                preferred_element_type=jnp.float32)
                new.append((m_new, l_new, acc_new))
            return tuple(new)

        init = tuple((jnp.full((Q_TILE, 1), NEG_BIG, jnp.float32),
                      jnp.zeros((Q_TILE, 1), jnp.float32),
                      jnp.zeros((Q_TILE, V_DIM), jnp.float32)) for _ in range(2))
        state = lax.fori_loop(0, qi, lambda kj, st: step(kj, st, False), init)
        state = step(qi, state, True)
        (_, l0, a0), (_, l1, a1) = state
        o = a0 / l0 - lam * (a1 / l1)
        y = o * lax.rsqrt(jnp.mean(o * o, axis=-1, keepdims=True) + LN_EPS)
        y = y * g_ref[...] * (1.0 - LAMBDA_INIT)
        o_ref[0, pl.ds(q0, Q_TILE), :] = y.astype(o_ref.dtype)
        return carry

    lax.fori_loop(0, nq, q_body, 0)


def _attn_call(lam_params, subln_g, q, k, v):
    B, S, _ = q.shape
    head = lambda b, h: (b, 0, h)
    const = lambda b, h: (0, 0)
    return pl.pallas_call(
        _attn_kernel,
        grid=(B, N_HEADS),
        in_specs=[pl.BlockSpec((4, HEAD_DIM), const),
                  pl.BlockSpec((1, V_DIM), const),
                  pl.BlockSpec((1, S, LANES), head),
                  pl.BlockSpec((1, S, LANES), head),
                  pl.BlockSpec((1, S, LANES), head)],
        out_specs=pl.BlockSpec((1, S, LANES), head),
        out_shape=jax.ShapeDtypeStruct((B, S, D_MODEL), jnp.float32),
        compiler_params=pltpu.CompilerParams(
            dimension_semantics=("arbitrary", "arbitrary"), vmem_limit_bytes=VMEM_LIMIT),
        name="diff_attn",
    )(lam_params, subln_g, q, k, v)


def _conv_kernel(x_ref, wg_ref, bg_ref, dwk_ref, dwb_ref, lng_ref, lnb_ref,
                 wp_ref, bp_ref, o_ref, ext_ref):
    ts = x_ref.shape[1]

    @pl.when(pl.program_id(1) == 0)
    def _():
        ext_ref[0:CONV_HALO, :] = jnp.zeros((CONV_HALO, D_MODEL), jnp.float32)

    xb = x_ref[0].astype(jnp.bfloat16)
    glu = jnp.dot(xb, wg_ref[...], preferred_element_type=jnp.float32) + bg_ref[...]
    u = glu[:, :D_MODEL] * jax.nn.sigmoid(glu[:, D_MODEL:])
    ext_ref[CONV_HALO:CONV_HALO + ts, :] = u
    base = CONV_HALO - (CONV_WIDTH - 1)
    acc = jnp.zeros((ts, D_MODEL), jnp.float32) + dwb_ref[...]
    for j in range(CONV_WIDTH):
        acc = acc + ext_ref[base + j:base + j + ts, :] * dwk_ref[j:j + 1, :]
    tail = ext_ref[ts:ts + CONV_HALO, :]
    ext_ref[0:CONV_HALO, :] = tail
    y = _layer_norm(acc, lng_ref[...], lnb_ref[...])
    y = y * jax.nn.sigmoid(y)
    o_ref[0] = (jnp.dot(y.astype(jnp.bfloat16), wp_ref[...],
                        preferred_element_type=jnp.float32) + bp_ref[...])


def _conv_call(x, w_glu, b_glu, dw_kernel, dw_bias, ln_g, ln_b, w_pw2, b_pw2):
    B, S, _ = x.shape
    ts = ROW_TILE
    row = lambda b, i: (b, i, 0)
    const = lambda b, i: (0, 0)
    vec = pl.BlockSpec((1, D_MODEL), const)
    return pl.pallas_call(
        _conv_kernel,
        grid=(B, S // ts),
        in_specs=[pl.BlockSpec((1, ts, D_MODEL), row),
                  pl.BlockSpec((D_MODEL, 2 * D_MODEL), const),
                  pl.BlockSpec((1, 2 * D_MODEL), const),
                  pl.BlockSpec((CONV_WIDTH, D_MODEL), const),
                  vec, vec, vec,
                  pl.BlockSpec((D_MODEL, D_MODEL), const),
                  vec],
        out_specs=pl.BlockSpec((1, ts, D_MODEL), row),
        out_shape=jax.ShapeDtypeStruct((B, S, D_MODEL), jnp.float32),
        scratch_shapes=[pltpu.VMEM((ROW_TILE + CONV_HALO, D_MODEL), jnp.float32)],
        compiler_params=pltpu.CompilerParams(
            dimension_semantics=("arbitrary", "arbitrary"), vmem_limit_bytes=VMEM_LIMIT),
        name="conv_branch",
    )(x, w_glu, b_glu, dw_kernel, dw_bias, ln_g, ln_b, w_pw2, b_pw2)


def _merge_kernel(x_ref, att_ref, conv_ref, wgate_ref, bgate_ref, wout_ref,
                  lng_ref, lnb_ref, o_ref):
    x = x_ref[...]
    gates = jnp.dot(x.astype(jnp.bfloat16), wgate_ref[...],
                    preferred_element_type=jnp.float32) + bgate_ref[...]
    g = jax.nn.sigmoid(gates)
    mix = g[:, :D_MODEL] * att_ref[...] + g[:, D_MODEL:] * conv_ref[...]
    mixed = jnp.dot(mix.astype(jnp.bfloat16), wout_ref[...],
                    preferred_element_type=jnp.float32)
    o_ref[...] = _layer_norm(DEEPNORM_ALPHA * x + mixed, lng_ref[...], lnb_ref[...])


def _merge_call(x2, att2, conv2, w_gate, b_gate, w_out, ln_g, ln_b):
    T = x2.shape[0]
    tm = ROW_TILE
    row = lambda i: (i, 0)
    const = lambda i: (0, 0)
    tile = pl.BlockSpec((tm, D_MODEL), row)
    vec = pl.BlockSpec((1, D_MODEL), const)
    return pl.pallas_call(
        _merge_kernel,
        grid=(T // tm,),
        in_specs=[tile, tile, tile,
                  pl.BlockSpec((D_MODEL, 2 * D_MODEL), const),
                  pl.BlockSpec((1, 2 * D_MODEL), const),
                  pl.BlockSpec((D_MODEL, D_MODEL), const),
                  vec, vec],
        out_specs=tile,
        out_shape=jax.ShapeDtypeStruct((T, D_MODEL), jnp.float32),
        compiler_params=pltpu.CompilerParams(
            dimension_semantics=("arbitrary",), vmem_limit_bytes=VMEM_LIMIT),
        name="gated_merge",
    )(x2, att2, conv2, w_gate, b_gate, w_out, ln_g, ln_b)


def _mlp_kernel(h_ref, w1_ref, w2_ref, lng_ref, lnb_ref, o_ref):
    h = h_ref[...]
    hb = h.astype(jnp.bfloat16)
    ff = None
    for c in range(D_FF // D_MODEL):
        a = jnp.dot(hb, w1_ref[:, c * D_MODEL:(c + 1) * D_MODEL],
                    preferred_element_type=jnp.float32)
        a = jnp.maximum(a, 0.0)
        a = (a * a).astype(jnp.bfloat16)
        part = jnp.dot(a, w2_ref[c * D_MODEL:(c + 1) * D_MODEL, :],
                       preferred_element_type=jnp.float32)
        ff = part if ff is None else ff + part
    o_ref[...] = _layer_norm(DEEPNORM_ALPHA * h + ff, lng_ref[...], lnb_ref[...])


def _mlp_call(h1, w_ff1, w_ff2, ln_g, ln_b):
    T = h1.shape[0]
    tm = ROW_TILE
    row = lambda i: (i, 0)
    const = lambda i: (0, 0)
    tile = pl.BlockSpec((tm, D_MODEL), row)
    vec = pl.BlockSpec((1, D_MODEL), const)
    return pl.pallas_call(
        _mlp_kernel,
        grid=(T // tm,),
        in_specs=[tile,
                  pl.BlockSpec((D_MODEL, D_FF), const),
                  pl.BlockSpec((D_FF, D_MODEL), const),
                  vec, vec],
        out_specs=tile,
        out_shape=jax.ShapeDtypeStruct((T, D_MODEL), jnp.float32),
        compiler_params=pltpu.CompilerParams(
            dimension_semantics=("arbitrary",), vmem_limit_bytes=VMEM_LIMIT),
        name="mlp_relu2",
    )(h1, w_ff1, w_ff2, ln_g, ln_b)


def kernel(x, positions, w_in, b_glu, b_gate, lambda_q1, lambda_k1, lambda_q2, lambda_k2, subln_g, dw_kernel, dw_bias, conv_ln_g, conv_ln_b, w_pw2, b_pw2, w_out, ln1_g, ln1_b, w_ff1, w_ff2, ln2_g, ln2_b):
    B, S, D = x.shape
    T = B * S
    bf16 = jnp.bfloat16
    half = HEAD_DIM // 2
    inv_freq = ROPE_THETA ** (-jnp.arange(half, dtype=jnp.float32) * 2.0 / HEAD_DIM)
    inv_lane = jnp.tile(inv_freq, LANES // half)[None, :]
    pos2 = positions.astype(jnp.float32).reshape(T, 1)
    x2 = x.reshape(T, D)

    l = 0
    w = w_in[l].astype(bf16)
    w_qkv = w[:, :3 * D]
    w_glu = w[:, 3 * D:5 * D]
    w_gate = w[:, 5 * D:]
    row = lambda a: a[l][None, :]

    q, k, v = _qkv_call(x2, pos2, inv_lane, w_qkv)
    lam_params = jnp.stack([lambda_q1[l], lambda_k1[l], lambda_q2[l], lambda_k2[l]])
    att = _attn_call(lam_params, row(subln_g),
                     q.reshape(B, S, D), k.reshape(B, S, D), v.reshape(B, S, D))
    conv = _conv_call(x, w_glu, row(b_glu), dw_kernel[l], row(dw_bias),
                      row(conv_ln_g), row(conv_ln_b), w_pw2[l].astype(bf16), row(b_pw2))
    h1 = _merge_call(x2, att.reshape(T, D), conv.reshape(T, D), w_gate, row(b_gate),
                     w_out[l].astype(bf16), row(ln1_g), row(ln1_b))
    out = _mlp_call(h1, w_ff1[l].astype(bf16), w_ff2[l].astype(bf16),
                    row(ln2_g), row(ln2_b))
    return out.reshape(B, S, D)
```

```python
import math

import jax
import jax.numpy as jnp
from jax import lax
from jax.experimental import pallas as pl
from jax.experimental.pallas import tpu as pltpu

D_MODEL = 1024
N_HEADS = 8
HEAD_DIM = 64
HALF = HEAD_DIM // 2
V_DIM = 2 * HEAD_DIM
CHUNK = 64
CONV_WIDTH = 31
D_FF = 4 * D_MODEL
ROPE_THETA = 10000.0
LN_EPS = 1e-5
DEPTH = 1
DEEPNORM_ALPHA = (2.0 * DEPTH) ** 0.25
LAMBDA_INIT = 0.8 - 0.6 * math.exp(-0.3 * 0)

LANES = 128
VMEM_LIMIT = 56 * 1024 * 1024

ROW_TILE = 512
Q_TILE = 512
K_TILE = 256
KV_UNROLL = 4
ONES_ROWS = 16
CONV_HALO = 32
NEG_BIG = -1e30
LOG2_E = math.log2(math.e)

_NT = (((1,), (1,)), ((), ()))


def _layer_norm(y, g, b):
    mu = jnp.mean(y, axis=-1, keepdims=True)
    d = y - mu
    var = jnp.mean(d * d, axis=-1, keepdims=True)
    return d * lax.rsqrt(var + LN_EPS) * g + b


def _rope_rows(t, cos, sin):
    pieces = []
    for g in range(D_MODEL // HEAD_DIM):
        t1 = t[g * HEAD_DIM:g * HEAD_DIM + HALF]
        t2 = t[g * HEAD_DIM + HALF:(g + 1) * HEAD_DIM]
        pieces.append(t1 * cos - t2 * sin)
        pieces.append(t1 * sin + t2 * cos)
    return jnp.concatenate(pieces, axis=0)


def _qkv_kernel(x_ref, pos_ref, inv_ref, wt_ref, qt_ref, k_ref, vt_ref):
    tm = x_ref.shape[1]
    xb = x_ref[0].astype(jnp.bfloat16)
    ang = inv_ref[...] * pos_ref[0]
    cos = jnp.cos(ang)
    sin = jnp.sin(ang)
    scale = HEAD_DIM ** -0.5 * LOG2_E

    qt = lax.dot_general(wt_ref[0:D_MODEL, :], xb, _NT,
                         preferred_element_type=jnp.float32)
    qt = _rope_rows(qt, cos * scale, sin * scale).astype(qt_ref.dtype)
    for j in range(tm // Q_TILE):
        qt_ref[0, j] = qt[:, j * Q_TILE:(j + 1) * Q_TILE]

    kt = lax.dot_general(wt_ref[D_MODEL:2 * D_MODEL, :], xb, _NT,
                         preferred_element_type=jnp.float32)
    k_ref[0] = _rope_rows(kt, cos, sin).T.astype(k_ref.dtype)

    vt = lax.dot_general(wt_ref[2 * D_MODEL:3 * D_MODEL, :], xb, _NT,
                         preferred_element_type=jnp.float32).astype(vt_ref.dtype)
    for j in range(tm // K_TILE):
        vt_ref[0, j] = vt[:, j * K_TILE:(j + 1) * K_TILE]


def _qkv_call(x, pos3, inv_col, wt_qkv):
    B, S, _ = x.shape
    tm = ROW_TILE
    const = lambda b, i: (0, 0)
    return pl.pallas_call(
        _qkv_kernel,
        grid=(B, S // tm),
        in_specs=[pl.BlockSpec((1, tm, D_MODEL), lambda b, i: (b, i, 0)),
                  pl.BlockSpec((1, 1, tm), lambda b, i: (b, 0, i)),
                  pl.BlockSpec((HALF, 1), const),
                  pl.BlockSpec((3 * D_MODEL, D_MODEL), const)],
        out_specs=[pl.BlockSpec((1, tm // Q_TILE, D_MODEL, Q_TILE), lambda b, i: (b, i, 0, 0)),
                   pl.BlockSpec((1, tm, D_MODEL), lambda b, i: (b, i, 0)),
                   pl.BlockSpec((1, tm // K_TILE, D_MODEL, K_TILE), lambda b, i: (b, i, 0, 0))],
        out_shape=[jax.ShapeDtypeStruct((B, S // Q_TILE, D_MODEL, Q_TILE), jnp.bfloat16),
                   jax.ShapeDtypeStruct((B, S, D_MODEL), jnp.bfloat16),
                   jax.ShapeDtypeStruct((B, S // K_TILE, D_MODEL, K_TILE), jnp.bfloat16)],
        compiler_params=pltpu.CompilerParams(
            dimension_semantics=("arbitrary", "arbitrary"), vmem_limit_bytes=VMEM_LIMIT),
        name="qkv_rope",
    )(x, pos3, inv_col, wt_qkv)


def _attn_kernel(lam_ref, g_ref, qt_ref, k_ref, vt_ref, o_ref, acc_ref):
    nq = qt_ref.shape[1]
    tiles_per_q = Q_TILE // K_TILE
    assert KV_UNROLL in (tiles_per_q, 2 * tiles_per_q)
    lp = lam_ref[...]
    lam = (jnp.exp(jnp.sum(lp[0:1] * lp[1:2], axis=-1, keepdims=True))
           - jnp.exp(jnp.sum(lp[2:3] * lp[3:4], axis=-1, keepdims=True))
           + LAMBDA_INIT)
    key_chunk = lax.broadcasted_iota(jnp.int32, (K_TILE, Q_TILE), 0) // CHUNK
    qry_chunk = lax.broadcasted_iota(jnp.int32, (K_TILE, Q_TILE), 1) // CHUNK
    diag_masks = tuple(key_chunk + (d * K_TILE) // CHUNK <= qry_chunk
                       for d in range(tiles_per_q))
    ones_rows = jnp.ones((ONES_ROWS, K_TILE), jnp.bfloat16)

    def q_body(qi, carry):
        qt = qt_ref[0, qi]
        zeros = jnp.zeros((HEAD_DIM, Q_TILE), qt.dtype)
        qz = (jnp.concatenate([qt[:HEAD_DIM], zeros], axis=0),
              jnp.concatenate([zeros, qt[HEAD_DIM:]], axis=0))
        acc_ref[...] = jnp.zeros_like(acc_ref)

        def steps(kj0, stats, masks):
            tiles = []
            for u in range(len(masks)):
                kj = kj0 + u
                k0 = pl.multiple_of(kj * K_TILE, K_TILE)
                kt = k_ref[0, pl.ds(k0, K_TILE), :]
                ss = [jnp.dot(kt, qz[c], preferred_element_type=jnp.float32) for c in range(2)]
                tiles.append((ss, jnp.concatenate([vt_ref[0, kj], ones_rows], axis=0)))
            stats = list(stats)
            for (ss, vt1), mask in zip(tiles, masks):
                for c in range(2):
                    m, l = stats[c]
                    s = ss[c]
                    if mask is not None:
                        s = jnp.where(mask, s, NEG_BIG)
                    m_new = jnp.maximum(m, jnp.max(s, axis=0, keepdims=True))
                    p = jnp.exp2(s - m_new)
                    alpha = jnp.exp2(m - m_new)
                    pv = jnp.dot(vt1, p.astype(jnp.bfloat16),
                                 preferred_element_type=jnp.float32)
                    l_new = alpha * l + pv[V_DIM:V_DIM + 1]
                    acc_ref[c] = alpha * acc_ref[c] + pv[:V_DIM]
                    stats[c] = (m_new, l_new)
            return tuple(stats)

        init = tuple((jnp.full((1, Q_TILE), NEG_BIG, jnp.float32),
                      jnp.zeros((1, Q_TILE), jnp.float32)) for _ in range(2))
        n_unmasked = qi * tiles_per_q
        n_full = n_unmasked // KV_UNROLL
        stats = lax.fori_loop(
            0, n_full, lambda it, st: steps(it * KV_UNROLL, st, (None,) * KV_UNROLL), init)
        stats = lax.cond(n_unmasked > n_full * KV_UNROLL,
                         lambda st: steps(n_full * KV_UNROLL, st, (None,) * tiles_per_q),
                         lambda st: st, stats)
        (_, l0), (_, l1) = steps(n_unmasked, stats, diag_masks)
        o = acc_ref[0] * (1.0 / l0) - acc_ref[1] * (lam / l1)
        y = o * lax.rsqrt(jnp.mean(o * o, axis=0, keepdims=True) + LN_EPS)
        q0 = pl.multiple_of(qi * Q_TILE, Q_TILE)
        o_ref[0, pl.ds(q0, Q_TILE), :] = y.T * (g_ref[...] * (1.0 - LAMBDA_INIT))
        return carry

    lax.fori_loop(0, nq, q_body, 0)


def _attn_call(lam_params, subln_g, qt, k, vt):
    B, S, _ = k.shape
    const = lambda b, h: (0, 0)
    return pl.pallas_call(
        _attn_kernel,
        grid=(B, N_HEADS),
        in_specs=[pl.BlockSpec((4, HEAD_DIM), const),
                  pl.BlockSpec((1, V_DIM), const),
                  pl.BlockSpec((1, S // Q_TILE, V_DIM, Q_TILE), lambda b, h: (b, 0, h, 0)),
                  pl.BlockSpec((1, S, LANES), lambda b, h: (b, 0, h)),
                  pl.BlockSpec((1, S // K_TILE, V_DIM, K_TILE), lambda b, h: (b, 0, h, 0))],
        out_specs=pl.BlockSpec((1, S, LANES), lambda b, h: (b, 0, h)),
        out_shape=jax.ShapeDtypeStruct((B, S, D_MODEL), jnp.float32),
        scratch_shapes=[pltpu.VMEM((2, V_DIM, Q_TILE), jnp.float32)],
        compiler_params=pltpu.CompilerParams(
            dimension_semantics=("arbitrary", "arbitrary"), vmem_limit_bytes=VMEM_LIMIT),
        name="diff_attn",
    )(lam_params, subln_g, qt, k, vt)


def _conv_kernel(x_ref, wg_ref, bg_ref, dwk_ref, dwb_ref, lng_ref, lnb_ref,
                 wp_ref, bp_ref, o_ref, ext_ref):
    ts = x_ref.shape[1]

    @pl.when(pl.program_id(1) == 0)
    def _():
        ext_ref[0:CONV_HALO, :] = jnp.zeros((CONV_HALO, D_MODEL), jnp.float32)

    xb = x_ref[0].astype(jnp.bfloat16)
    glu = jnp.dot(xb, wg_ref[...], preferred_element_type=jnp.float32) + bg_ref[...]
    u = glu[:, :D_MODEL] * jax.nn.sigmoid(glu[:, D_MODEL:])
    ext_ref[CONV_HALO:CONV_HALO + ts, :] = u
    base = CONV_HALO - (CONV_WIDTH - 1)
    acc = jnp.zeros((ts, D_MODEL), jnp.float32) + dwb_ref[...]
    for j in range(CONV_WIDTH):
        acc = acc + ext_ref[base + j:base + j + ts, :] * dwk_ref[j:j + 1, :]
    tail = ext_ref[ts:ts + CONV_HALO, :]
    ext_ref[0:CONV_HALO, :] = tail
    y = _layer_norm(acc, lng_ref[...], lnb_ref[...])
    y = y * jax.nn.sigmoid(y)
    o_ref[0] = (jnp.dot(y.astype(jnp.bfloat16), wp_ref[...],
                        preferred_element_type=jnp.float32) + bp_ref[...])


def _conv_call(x, w_glu, b_glu, dw_kernel, dw_bias, ln_g, ln_b, w_pw2, b_pw2):
    B, S, _ = x.shape
    ts = ROW_TILE
    row = lambda b, i: (b, i, 0)
    const = lambda b, i: (0, 0)
    vec = pl.BlockSpec((1, D_MODEL), const)
    return pl.pallas_call(
        _conv_kernel,
        grid=(B, S // ts),
        in_specs=[pl.BlockSpec((1, ts, D_MODEL), row),
                  pl.BlockSpec((D_MODEL, 2 * D_MODEL), const),
                  pl.BlockSpec((1, 2 * D_MODEL), const),
                  pl.BlockSpec((CONV_WIDTH, D_MODEL), const),
                  vec, vec, vec,
                  pl.BlockSpec((D_MODEL, D_MODEL), const),
                  vec],
        out_specs=pl.BlockSpec((1, ts, D_MODEL), row),
        out_shape=jax.ShapeDtypeStruct((B, S, D_MODEL), jnp.float32),
        scratch_shapes=[pltpu.VMEM((ROW_TILE + CONV_HALO, D_MODEL), jnp.float32)],
        compiler_params=pltpu.CompilerParams(
            dimension_semantics=("arbitrary", "arbitrary"), vmem_limit_bytes=VMEM_LIMIT),
        name="conv_branch",
    )(x, w_glu, b_glu, dw_kernel, dw_bias, ln_g, ln_b, w_pw2, b_pw2)


def _merge_kernel(x_ref, att_ref, conv_ref, wgate_ref, bgate_ref, wout_ref,
                  lng_ref, lnb_ref, o_ref):
    x = x_ref[...]
    gates = jnp.dot(x.astype(jnp.bfloat16), wgate_ref[...],
                    preferred_element_type=jnp.float32) + bgate_ref[...]
    g = jax.nn.sigmoid(gates)
    mix = g[:, :D_MODEL] * att_ref[...] + g[:, D_MODEL:] * conv_ref[...]
    mixed = jnp.dot(mix.astype(jnp.bfloat16), wout_ref[...],
                    preferred_element_type=jnp.float32)
    o_ref[...] = _layer_norm(DEEPNORM_ALPHA * x + mixed, lng_ref[...], lnb_ref[...])


def _merge_call(x2, att2, conv2, w_gate, b_gate, w_out, ln_g, ln_b):
    T = x2.shape[0]
    tm = ROW_TILE
    row = lambda i: (i, 0)
    const = lambda i: (0, 0)
    tile = pl.BlockSpec((tm, D_MODEL), row)
    vec = pl.BlockSpec((1, D_MODEL), const)
    return pl.pallas_call(
        _merge_kernel,
        grid=(T // tm,),
        in_specs=[tile, tile, tile,
                  pl.BlockSpec((D_MODEL, 2 * D_MODEL), const),
                  pl.BlockSpec((1, 2 * D_MODEL), const),
                  pl.BlockSpec((D_MODEL, D_MODEL), const),
                  vec, vec],
        out_specs=tile,
        out_shape=jax.ShapeDtypeStruct((T, D_MODEL), jnp.float32),
        compiler_params=pltpu.CompilerParams(
            dimension_semantics=("arbitrary",), vmem_limit_bytes=VMEM_LIMIT),
        name="gated_merge",
    )(x2, att2, conv2, w_gate, b_gate, w_out, ln_g, ln_b)


def _mlp_kernel(h_ref, w1_ref, w2_ref, lng_ref, lnb_ref, o_ref):
    h = h_ref[...]
    hb = h.astype(jnp.bfloat16)
    ff = None
    for c in range(D_FF // D_MODEL):
        a = jnp.dot(hb, w1_ref[:, c * D_MODEL:(c + 1) * D_MODEL],
                    preferred_element_type=jnp.float32)
        a = jnp.maximum(a, 0.0)
        a = (a * a).astype(jnp.bfloat16)
        part = jnp.dot(a, w2_ref[c * D_MODEL:(c + 1) * D_MODEL, :],
                       preferred_element_type=jnp.float32)
        ff = part if ff is None else ff + part
    o_ref[...] = _layer_norm(DEEPNORM_ALPHA * h + ff, lng_ref[...], lnb_ref[...])


def _mlp_call(h1, w_ff1, w_ff2, ln_g, ln_b):
    T = h1.shape[0]
    tm = ROW_TILE
    row = lambda i: (i, 0)
    const = lambda i: (0, 0)
    tile = pl.BlockSpec((tm, D_MODEL), row)
    vec = pl.BlockSpec((1, D_MODEL), const)
    return pl.pallas_call(
        _mlp_kernel,
        grid=(T // tm,),
        in_specs=[tile,
                  pl.BlockSpec((D_MODEL, D_FF), const),
                  pl.BlockSpec((D_FF, D_MODEL), const),
                  vec, vec],
        out_specs=tile,
        out_shape=jax.ShapeDtypeStruct((T, D_MODEL), jnp.float32),
        compiler_params=pltpu.CompilerParams(
            dimension_semantics=("arbitrary",), vmem_limit_bytes=VMEM_LIMIT),
        name="mlp_relu2",
    )(h1, w_ff1, w_ff2, ln_g, ln_b)


def kernel(x, positions, w_in, b_glu, b_gate, lambda_q1, lambda_k1, lambda_q2, lambda_k2, subln_g, dw_kernel, dw_bias, conv_ln_g, conv_ln_b, w_pw2, b_pw2, w_out, ln1_g, ln1_b, w_ff1, w_ff2, ln2_g, ln2_b):
    B, S, D = x.shape
    T = B * S
    bf16 = jnp.bfloat16
    inv_freq = ROPE_THETA ** (-jnp.arange(HALF, dtype=jnp.float32) * 2.0 / HEAD_DIM)
    inv_col = inv_freq[:, None]
    pos3 = positions.astype(jnp.float32).reshape(B, 1, S)
    x2 = x.reshape(T, D)

    l = 0
    w = w_in[l].astype(bf16)
    wt_qkv = w[:, :3 * D].T
    w_glu = w[:, 3 * D:5 * D]
    w_gate = w[:, 5 * D:]
    row = lambda a: a[l][None, :]

    qt, k, vt = _qkv_call(x, pos3, inv_col, wt_qkv)
    lam_params = jnp.stack([lambda_q1[l], lambda_k1[l], lambda_q2[l], lambda_k2[l]])
    att = _attn_call(lam_params, row(subln_g), qt, k, vt)
    conv = _conv_call(x, w_glu, row(b_glu), dw_kernel[l], row(dw_bias),
                      row(conv_ln_g), row(conv_ln_b), w_pw2[l].astype(bf16), row(b_pw2))
    h1 = _merge_call(x2, att.reshape(T, D), conv.reshape(T, D), w_gate, row(b_gate),
                     w_out[l].astype(bf16), row(ln1_g), row(ln1_b))
    out = _mlp_call(h1, w_ff1[l].astype(bf16), w_ff2[l].astype(bf16),
                    row(ln2_g), row(ln2_b))
    return out.reshape(B, S, D)
```

```python
import math

import jax
import jax.numpy as jnp
from jax import lax
from jax.experimental import pallas as pl
from jax.experimental.pallas import tpu as pltpu

D_MODEL = 1024
N_HEADS = 8
HEAD_DIM = 64
HALF = HEAD_DIM // 2
V_DIM = 2 * HEAD_DIM
CHUNK = 64
CONV_WIDTH = 31
D_FF = 4 * D_MODEL
ROPE_THETA = 10000.0
LN_EPS = 1e-5
DEPTH = 1
DEEPNORM_ALPHA = (2.0 * DEPTH) ** 0.25
LAMBDA_INIT = 0.8 - 0.6 * math.exp(-0.3 * 0)

LANES = 128
VMEM_LIMIT = 56 * 1024 * 1024

ROW_TILE = 512
Q_TILE = 512
K_TILE = 256
KV_UNROLL = 4
ONES_ROWS = 16
CONV_HALO = 32
CONV_PAD = 8
CONV_ROWS = 64
NEG_BIG = -1e30
LOG2_E = math.log2(math.e)

_NT = (((1,), (1,)), ((), ()))


def _layer_norm(y, g, b):
    mu = jnp.mean(y, axis=-1, keepdims=True)
    d = y - mu
    var = jnp.mean(d * d, axis=-1, keepdims=True)
    return d * lax.rsqrt(var + LN_EPS) * g + b


def _rope_rows(t, cos, sin):
    pieces = []
    for g in range(D_MODEL // HEAD_DIM):
        t1 = t[g * HEAD_DIM:g * HEAD_DIM + HALF]
        t2 = t[g * HEAD_DIM + HALF:(g + 1) * HEAD_DIM]
        pieces.append(t1 * cos - t2 * sin)
        pieces.append(t1 * sin + t2 * cos)
    return jnp.concatenate(pieces, axis=0)


def _qkv_kernel(x_ref, pos_ref, inv_ref, wt_ref, qt_ref, k_ref, vt_ref):
    tm = x_ref.shape[1]
    xb = x_ref[0].astype(jnp.bfloat16)
    ang = inv_ref[...] * pos_ref[0]
    cos = jnp.cos(ang)
    sin = jnp.sin(ang)
    scale = HEAD_DIM ** -0.5 * LOG2_E

    qt = lax.dot_general(wt_ref[0:D_MODEL, :], xb, _NT,
                         preferred_element_type=jnp.float32)
    qt = _rope_rows(qt, cos * scale, sin * scale).astype(qt_ref.dtype)
    for j in range(tm // Q_TILE):
        qt_ref[0, j] = qt[:, j * Q_TILE:(j + 1) * Q_TILE]

    kt = lax.dot_general(wt_ref[D_MODEL:2 * D_MODEL, :], xb, _NT,
                         preferred_element_type=jnp.float32)
    k_ref[0] = _rope_rows(kt, cos, sin).T.astype(k_ref.dtype)

    vt = lax.dot_general(wt_ref[2 * D_MODEL:3 * D_MODEL, :], xb, _NT,
                         preferred_element_type=jnp.float32).astype(vt_ref.dtype)
    for j in range(tm // K_TILE):
        vt_ref[0, j] = vt[:, j * K_TILE:(j + 1) * K_TILE]


def _qkv_call(x, pos3, inv_col, wt_qkv):
    B, S, _ = x.shape
    tm = ROW_TILE
    const = lambda b, i: (0, 0)
    return pl.pallas_call(
        _qkv_kernel,
        grid=(B, S // tm),
        in_specs=[pl.BlockSpec((1, tm, D_MODEL), lambda b, i: (b, i, 0)),
                  pl.BlockSpec((1, 1, tm), lambda b, i: (b, 0, i)),
                  pl.BlockSpec((HALF, 1), const),
                  pl.BlockSpec((3 * D_MODEL, D_MODEL), const)],
        out_specs=[pl.BlockSpec((1, tm // Q_TILE, D_MODEL, Q_TILE), lambda b, i: (b, i, 0, 0)),
                   pl.BlockSpec((1, tm, D_MODEL), lambda b, i: (b, i, 0)),
                   pl.BlockSpec((1, tm // K_TILE, D_MODEL, K_TILE), lambda b, i: (b, i, 0, 0))],
        out_shape=[jax.ShapeDtypeStruct((B, S // Q_TILE, D_MODEL, Q_TILE), jnp.bfloat16),
                   jax.ShapeDtypeStruct((B, S, D_MODEL), jnp.bfloat16),
                   jax.ShapeDtypeStruct((B, S // K_TILE, D_MODEL, K_TILE), jnp.bfloat16)],
        compiler_params=pltpu.CompilerParams(
            dimension_semantics=("arbitrary", "arbitrary"), vmem_limit_bytes=VMEM_LIMIT),
        name="qkv_rope",
    )(x, pos3, inv_col, wt_qkv)


def _attn_kernel(lam_ref, g_ref, qt_ref, k_ref, vt_ref, o_ref, acc_ref):
    nq = qt_ref.shape[1]
    tiles_per_q = Q_TILE // K_TILE
    assert KV_UNROLL in (tiles_per_q, 2 * tiles_per_q)
    lp = lam_ref[...]
    lam = (jnp.exp(jnp.sum(lp[0:1] * lp[1:2], axis=-1, keepdims=True))
           - jnp.exp(jnp.sum(lp[2:3] * lp[3:4], axis=-1, keepdims=True))
           + LAMBDA_INIT)
    key_chunk = lax.broadcasted_iota(jnp.int32, (K_TILE, Q_TILE), 0) // CHUNK
    qry_chunk = lax.broadcasted_iota(jnp.int32, (K_TILE, Q_TILE), 1) // CHUNK
    diag_masks = tuple(key_chunk + (d * K_TILE) // CHUNK <= qry_chunk
                       for d in range(tiles_per_q))
    ones_rows = jnp.ones((ONES_ROWS, K_TILE), jnp.bfloat16)

    def q_body(qi, carry):
        qt = qt_ref[0, qi]
        zeros = jnp.zeros((HEAD_DIM, Q_TILE), qt.dtype)
        qz = (jnp.concatenate([qt[:HEAD_DIM], zeros], axis=0),
              jnp.concatenate([zeros, qt[HEAD_DIM:]], axis=0))
        acc_ref[...] = jnp.zeros_like(acc_ref)

        def steps(kj0, stats, masks):
            tiles = []
            for u in range(len(masks)):
                kj = kj0 + u
                k0 = pl.multiple_of(kj * K_TILE, K_TILE)
                kt = k_ref[0, pl.ds(k0, K_TILE), :]
                ss = [jnp.dot(kt, qz[c], preferred_element_type=jnp.float32) for c in range(2)]
                tiles.append((ss, jnp.concatenate([vt_ref[0, kj], ones_rows], axis=0)))
            stats = list(stats)
            for (ss, vt1), mask in zip(tiles, masks):
                for c in range(2):
                    m, l = stats[c]
                    s = ss[c]
                    if mask is not None:
                        s = jnp.where(mask, s, NEG_BIG)
                    m_new = jnp.maximum(m, jnp.max(s, axis=0, keepdims=True))
                    p = jnp.exp2(s - m_new)
                    alpha = jnp.exp2(m - m_new)
                    pv = jnp.dot(vt1, p.astype(jnp.bfloat16),
                                 preferred_element_type=jnp.float32)
                    l_new = alpha * l + pv[V_DIM:V_DIM + 1]
                    acc_ref[c] = alpha * acc_ref[c] + pv[:V_DIM]
                    stats[c] = (m_new, l_new)
            return tuple(stats)

        init = tuple((jnp.full((1, Q_TILE), NEG_BIG, jnp.float32),
                      jnp.zeros((1, Q_TILE), jnp.float32)) for _ in range(2))
        n_unmasked = qi * tiles_per_q
        n_full = n_unmasked // KV_UNROLL
        stats = lax.fori_loop(
            0, n_full, lambda it, st: steps(it * KV_UNROLL, st, (None,) * KV_UNROLL), init)
        stats = lax.cond(n_unmasked > n_full * KV_UNROLL,
                         lambda st: steps(n_full * KV_UNROLL, st, (None,) * tiles_per_q),
                         lambda st: st, stats)
        (_, l0), (_, l1) = steps(n_unmasked, stats, diag_masks)
        o = acc_ref[0] * (1.0 / l0) - acc_ref[1] * (lam / l1)
        y = o * lax.rsqrt(jnp.mean(o * o, axis=0, keepdims=True) + LN_EPS)
        q0 = pl.multiple_of(qi * Q_TILE, Q_TILE)
        o_ref[0, pl.ds(q0, Q_TILE), :] = y.T * (g_ref[...] * (1.0 - LAMBDA_INIT))
        return carry

    lax.fori_loop(0, nq, q_body, 0)


def _attn_call(lam_params, subln_g, qt, k, vt):
    B, S, _ = k.shape
    const = lambda b, h: (0, 0)
    return pl.pallas_call(
        _attn_kernel,
        grid=(B, N_HEADS),
        in_specs=[pl.BlockSpec((4, HEAD_DIM), const),
                  pl.BlockSpec((1, V_DIM), const),
                  pl.BlockSpec((1, S // Q_TILE, V_DIM, Q_TILE), lambda b, h: (b, 0, h, 0)),
                  pl.BlockSpec((1, S, LANES), lambda b, h: (b, 0, h)),
                  pl.BlockSpec((1, S // K_TILE, V_DIM, K_TILE), lambda b, h: (b, 0, h, 0))],
        out_specs=pl.BlockSpec((1, S, LANES), lambda b, h: (b, 0, h)),
        out_shape=jax.ShapeDtypeStruct((B, S, D_MODEL), jnp.float32),
        scratch_shapes=[pltpu.VMEM((2, V_DIM, Q_TILE), jnp.float32)],
        compiler_params=pltpu.CompilerParams(
            dimension_semantics=("arbitrary", "arbitrary"), vmem_limit_bytes=VMEM_LIMIT),
        name="diff_attn",
    )(lam_params, subln_g, qt, k, vt)


def _depthwise_conv(ext_ref, dwk_ref, dwb_ref, out_ref, ts):
    base = CONV_HALO - (CONV_WIDTH - 1)
    groups = [[(j, (base + j) // 8) for j in range(CONV_WIDTH) if (base + j) % 8 == r]
              for r in range(8)]
    max_a = max(a for g in groups for _, a in g)
    rows = CONV_ROWS + CONV_PAD
    assert rows + 8 * max_a <= CONV_ROWS + CONV_HALO + CONV_PAD

    def col_body(c, carry):
        lanes = pl.ds(pl.multiple_of(c * LANES, LANES), LANES)
        kcol = dwk_ref[:, lanes]
        bias = dwb_ref[:, lanes]
        for rb in range(ts // CONV_ROWS):
            t0 = rb * CONV_ROWS
            e = ext_ref[t0:t0 + rows + 8 * max_a, lanes]
            w = None
            for r in range(7, -1, -1):
                v = None
                for j, a in groups[r]:
                    term = e[8 * a:8 * a + rows] * kcol[j:j + 1]
                    v = term if v is None else v + term
                if w is not None:
                    v = v + pltpu.roll(w, rows - 1, 0)
                w = v
            out_ref[t0:t0 + CONV_ROWS, lanes] = w[:CONV_ROWS] + bias
        return carry

    lax.fori_loop(0, D_MODEL // LANES, col_body, 0)


def _conv_kernel(x_ref, wg_ref, bg_ref, dwk_ref, dwb_ref, lng_ref, lnb_ref,
                 wp_ref, bp_ref, o_ref, ext_ref, acc_ref):
    ts = x_ref.shape[1]

    @pl.when(pl.program_id(1) == 0)
    def _():
        ext_ref[0:CONV_HALO, :] = jnp.zeros((CONV_HALO, D_MODEL), jnp.float32)

    xb = x_ref[0].astype(jnp.bfloat16)
    glu = jnp.dot(xb, wg_ref[...], preferred_element_type=jnp.float32) + bg_ref[...]
    ext_ref[CONV_HALO:CONV_HALO + ts, :] = glu[:, :D_MODEL] * jax.nn.sigmoid(glu[:, D_MODEL:])
    ext_ref[CONV_HALO + ts:, :] = jnp.zeros((CONV_PAD, D_MODEL), jnp.float32)
    _depthwise_conv(ext_ref, dwk_ref, dwb_ref, acc_ref, ts)
    ext_ref[0:CONV_HALO, :] = ext_ref[ts:ts + CONV_HALO, :]
    y = _layer_norm(acc_ref[...], lng_ref[...], lnb_ref[...])
    y = y * jax.nn.sigmoid(y)
    o_ref[0] = (jnp.dot(y.astype(jnp.bfloat16), wp_ref[...],
                        preferred_element_type=jnp.float32) + bp_ref[...])


def _conv_call(x, w_glu, b_glu, dw_kernel, dw_bias, ln_g, ln_b, w_pw2, b_pw2):
    B, S, _ = x.shape
    ts = ROW_TILE
    row = lambda b, i: (b, i, 0)
    const = lambda b, i: (0, 0)
    vec = pl.BlockSpec((1, D_MODEL), const)
    return pl.pallas_call(
        _conv_kernel,
        grid=(B, S // ts),
        in_specs=[pl.BlockSpec((1, ts, D_MODEL), row),
                  pl.BlockSpec((D_MODEL, 2 * D_MODEL), const),
                  pl.BlockSpec((1, 2 * D_MODEL), const),
                  pl.BlockSpec((CONV_WIDTH, D_MODEL), const),
                  vec, vec, vec,
                  pl.BlockSpec((D_MODEL, D_MODEL), const),
                  vec],
        out_specs=pl.BlockSpec((1, ts, D_MODEL), row),
        out_shape=jax.ShapeDtypeStruct((B, S, D_MODEL), jnp.float32),
        scratch_shapes=[pltpu.VMEM((ROW_TILE + CONV_HALO + CONV_PAD, D_MODEL), jnp.float32),
                        pltpu.VMEM((ROW_TILE, D_MODEL), jnp.float32)],
        compiler_params=pltpu.CompilerParams(
            dimension_semantics=("arbitrary", "arbitrary"), vmem_limit_bytes=VMEM_LIMIT),
        name="conv_branch",
    )(x, w_glu, b_glu, dw_kernel, dw_bias, ln_g, ln_b, w_pw2, b_pw2)


def _merge_kernel(x_ref, att_ref, conv_ref, wgate_ref, bgate_ref, wout_ref,
                  lng_ref, lnb_ref, o_ref):
    x = x_ref[...]
    gates = jnp.dot(x.astype(jnp.bfloat16), wgate_ref[...],
                    preferred_element_type=jnp.float32) + bgate_ref[...]
    g = jax.nn.sigmoid(gates)
    mix = g[:, :D_MODEL] * att_ref[...] + g[:, D_MODEL:] * conv_ref[...]
    mixed = jnp.dot(mix.astype(jnp.bfloat16), wout_ref[...],
                    preferred_element_type=jnp.float32)
    o_ref[...] = _layer_norm(DEEPNORM_ALPHA * x + mixed, lng_ref[...], lnb_ref[...])


def _merge_call(x2, att2, conv2, w_gate, b_gate, w_out, ln_g, ln_b):
    T = x2.shape[0]
    tm = ROW_TILE
    row = lambda i: (i, 0)
    const = lambda i: (0, 0)
    tile = pl.BlockSpec((tm, D_MODEL), row)
    vec = pl.BlockSpec((1, D_MODEL), const)
    return pl.pallas_call(
        _merge_kernel,
        grid=(T // tm,),
        in_specs=[tile, tile, tile,
                  pl.BlockSpec((D_MODEL, 2 * D_MODEL), const),
                  pl.BlockSpec((1, 2 * D_MODEL), const),
                  pl.BlockSpec((D_MODEL, D_MODEL), const),
                  vec, vec],
        out_specs=tile,
        out_shape=jax.ShapeDtypeStruct((T, D_MODEL), jnp.float32),
        compiler_params=pltpu.CompilerParams(
            dimension_semantics=("arbitrary",), vmem_limit_bytes=VMEM_LIMIT),
        name="gated_merge",
    )(x2, att2, conv2, w_gate, b_gate, w_out, ln_g, ln_b)


def _mlp_kernel(h_ref, w1_ref, w2_ref, lng_ref, lnb_ref, o_ref):
    h = h_ref[...]
    hb = h.astype(jnp.bfloat16)
    ff = None
    for c in range(D_FF // D_MODEL):
        a = jnp.dot(hb, w1_ref[:, c * D_MODEL:(c + 1) * D_MODEL],
                    preferred_element_type=jnp.float32)
        a = jnp.maximum(a, 0.0)
        a = (a * a).astype(jnp.bfloat16)
        part = jnp.dot(a, w2_ref[c * D_MODEL:(c + 1) * D_MODEL, :],
                       preferred_element_type=jnp.float32)
        ff = part if ff is None else ff + part
    o_ref[...] = _layer_norm(DEEPNORM_ALPHA * h + ff, lng_ref[...], lnb_ref[...])


def _mlp_call(h1, w_ff1, w_ff2, ln_g, ln_b):
    T = h1.shape[0]
    tm = ROW_TILE
    row = lambda i: (i, 0)
    const = lambda i: (0, 0)
    tile = pl.BlockSpec((tm, D_MODEL), row)
    vec = pl.BlockSpec((1, D_MODEL), const)
    return pl.pallas_call(
        _mlp_kernel,
        grid=(T // tm,),
        in_specs=[tile,
                  pl.BlockSpec((D_MODEL, D_FF), const),
                  pl.BlockSpec((D_FF, D_MODEL), const),
                  vec, vec],
        out_specs=tile,
        out_shape=jax.ShapeDtypeStruct((T, D_MODEL), jnp.float32),
        compiler_params=pltpu.CompilerParams(
            dimension_semantics=("arbitrary",), vmem_limit_bytes=VMEM_LIMIT),
        name="mlp_relu2",
    )(h1, w_ff1, w_ff2, ln_g, ln_b)


def kernel(x, positions, w_in, b_glu, b_gate, lambda_q1, lambda_k1, lambda_q2, lambda_k2, subln_g, dw_kernel, dw_bias, conv_ln_g, conv_ln_b, w_pw2, b_pw2, w_out, ln1_g, ln1_b, w_ff1, w_ff2, ln2_g, ln2_b):
    B, S, D = x.shape
    T = B * S
    bf16 = jnp.bfloat16
    inv_freq = ROPE_THETA ** (-jnp.arange(HALF, dtype=jnp.float32) * 2.0 / HEAD_DIM)
    inv_col = inv_freq[:, None]
    pos3 = positions.astype(jnp.float32).reshape(B, 1, S)
    x2 = x.reshape(T, D)

    l = 0
    w = w_in[l].astype(bf16)
    wt_qkv = w[:, :3 * D].T
    w_glu = w[:, 3 * D:5 * D]
    w_gate = w[:, 5 * D:]
    row = lambda a: a[l][None, :]

    qt, k, vt = _qkv_call(x, pos3, inv_col, wt_qkv)
    lam_params = jnp.stack([lambda_q1[l], lambda_k1[l], lambda_q2[l], lambda_k2[l]])
    att = _attn_call(lam_params, row(subln_g), qt, k, vt)
    conv = _conv_call(x, w_glu, row(b_glu), dw_kernel[l], row(dw_bias),
                      row(conv_ln_g), row(conv_ln_b), w_pw2[l].astype(bf16), row(b_pw2))
    h1 = _merge_call(x2, att.reshape(T, D), conv.reshape(T, D), w_gate, row(b_gate),
                     w_out[l].astype(bf16), row(ln1_g), row(ln1_b))
    out = _mlp_call(h1, w_ff1[l].astype(bf16), w_ff2[l].astype(bf16),
                    row(ln2_g), row(ln2_b))
    return out.reshape(B, S, D)
```

```python
import functools
import math

import jax
import jax.numpy as jnp
import numpy as np
from jax import lax
from jax.experimental import pallas as pl
from jax.experimental.pallas import tpu as pltpu

D_MODEL = 1024
N_HEADS = 8
HEAD_DIM = 64
HALF = HEAD_DIM // 2
V_DIM = 2 * HEAD_DIM
CHUNK = 64
CONV_WIDTH = 31
D_FF = 4 * D_MODEL
ROPE_THETA = 10000.0
LN_EPS = 1e-5
DEPTH = 1
DEEPNORM_ALPHA = (2.0 * DEPTH) ** 0.25
LAMBDA_INIT = 0.8 - 0.6 * math.exp(-0.3 * 0)

LANES = 128
VMEM_LIMIT = 56 * 1024 * 1024

ROW_TILE = 512
Q_TILE = 512
K_TILE = 256
KV_UNROLL = 6
ONES_ROWS = 16
PEN_ROWS = 16
CONV_HALO = 32
CONV_PAD = 8
CONV_ROWS = 64
NEG_BIG = -1e30
LOG2_E = math.log2(math.e)

_NT = (((1,), (1,)), ((), ()))


def _layer_norm(y, g, b):
    mu = jnp.mean(y, axis=-1, keepdims=True)
    d = y - mu
    var = jnp.mean(d * d, axis=-1, keepdims=True)
    return d * lax.rsqrt(var + LN_EPS) * g + b


def _rope_rows(t, cos, sin):
    pieces = []
    for g in range(D_MODEL // HEAD_DIM):
        t1 = t[g * HEAD_DIM:g * HEAD_DIM + HALF]
        t2 = t[g * HEAD_DIM + HALF:(g + 1) * HEAD_DIM]
        pieces.append(t1 * cos - t2 * sin)
        pieces.append(t1 * sin + t2 * cos)
    return jnp.concatenate(pieces, axis=0)


def _qkv_kernel(x_ref, pos_ref, inv_ref, wt_ref, qt_ref, k_ref, vt_ref):
    tm = x_ref.shape[1]
    xb = x_ref[0].astype(jnp.bfloat16)
    ang = inv_ref[...] * pos_ref[0]
    cos = jnp.cos(ang)
    sin = jnp.sin(ang)
    scale = HEAD_DIM ** -0.5 * LOG2_E

    qt = lax.dot_general(wt_ref[0:D_MODEL, :], xb, _NT,
                         preferred_element_type=jnp.float32)
    qt = _rope_rows(qt, cos * scale, sin * scale).astype(qt_ref.dtype)
    for j in range(tm // Q_TILE):
        qt_ref[0, j] = qt[:, j * Q_TILE:(j + 1) * Q_TILE]

    kt = lax.dot_general(wt_ref[D_MODEL:2 * D_MODEL, :], xb, _NT,
                         preferred_element_type=jnp.float32)
    k_ref[0] = _rope_rows(kt, cos, sin).T.astype(k_ref.dtype)

    vt = lax.dot_general(wt_ref[2 * D_MODEL:3 * D_MODEL, :], xb, _NT,
                         preferred_element_type=jnp.float32).astype(vt_ref.dtype)
    for j in range(tm // K_TILE):
        vt_ref[0, j] = vt[:, j * K_TILE:(j + 1) * K_TILE]


def _qkv_call(x, pos3, inv_col, wt_qkv):
    B, S, _ = x.shape
    tm = ROW_TILE
    const = lambda b, i: (0, 0)
    return pl.pallas_call(
        _qkv_kernel,
        grid=(B, S // tm),
        in_specs=[pl.BlockSpec((1, tm, D_MODEL), lambda b, i: (b, i, 0)),
                  pl.BlockSpec((1, 1, tm), lambda b, i: (b, 0, i)),
                  pl.BlockSpec((HALF, 1), const),
                  pl.BlockSpec((3 * D_MODEL, D_MODEL), const)],
        out_specs=[pl.BlockSpec((1, tm // Q_TILE, D_MODEL, Q_TILE), lambda b, i: (b, i, 0, 0)),
                   pl.BlockSpec((1, tm, D_MODEL), lambda b, i: (b, i, 0)),
                   pl.BlockSpec((1, tm // K_TILE, D_MODEL, K_TILE), lambda b, i: (b, i, 0, 0))],
        out_shape=[jax.ShapeDtypeStruct((B, S // Q_TILE, D_MODEL, Q_TILE), jnp.bfloat16),
                   jax.ShapeDtypeStruct((B, S, D_MODEL), jnp.bfloat16),
                   jax.ShapeDtypeStruct((B, S // K_TILE, D_MODEL, K_TILE), jnp.bfloat16)],
        compiler_params=pltpu.CompilerParams(
            dimension_semantics=("arbitrary", "arbitrary"), vmem_limit_bytes=VMEM_LIMIT),
        name="qkv_rope",
    )(x, pos3, inv_col, wt_qkv)


def _attn_schedule(nq):
    tiles_per_q = Q_TILE // K_TILE
    qs, ks = [], []
    for qi in range(nq):
        for kj in range((qi + 1) * tiles_per_q):
            qs.append(qi)
            ks.append(kj)
    return np.asarray(qs, np.int32), np.asarray(ks, np.int32)


def _attn_kernel(qtab_ref, ktab_ref, lam_ref, g_ref, qt_ref, k_ref, vt_ref, o_ref,
                 s_ref, m_ref, l_ref, acc_ref, *, n_steps, unroll):
    nq = qt_ref.shape[1]
    n_groups = n_steps // unroll
    chunks_per_k = K_TILE // CHUNK
    chunks_per_q = Q_TILE // CHUNK
    lp = lam_ref[...]
    lam = (jnp.exp(jnp.sum(lp[0:1] * lp[1:2], axis=-1, keepdims=True))
           - jnp.exp(jnp.sum(lp[2:3] * lp[3:4], axis=-1, keepdims=True))
           + LAMBDA_INIT)
    qry_chunk = lax.broadcasted_iota(jnp.int32, (1, Q_TILE), 1) // CHUNK
    pen_row = lax.broadcasted_iota(jnp.int32, (PEN_ROWS, Q_TILE), 0)
    key_row_chunk = lax.broadcasted_iota(jnp.int32, (K_TILE, LANES), 0) // CHUNK
    key_lane = lax.broadcasted_iota(jnp.int32, (K_TILE, LANES), 1)
    chunk_onehot = jnp.where(key_row_chunk == key_lane, 1.0, 0.0).astype(jnp.bfloat16)
    ones_rows = jnp.ones((ONES_ROWS, K_TILE), jnp.bfloat16)
    zeros = jnp.zeros((LANES, Q_TILE), jnp.bfloat16)

    m_ref[...] = jnp.full_like(m_ref, NEG_BIG)
    l_ref[...] = jnp.zeros_like(l_ref)
    acc_ref[...] = jnp.zeros_like(acc_ref)

    def scores(g, slot):
        for u in range(unroll):
            qi = qtab_ref[g * unroll + u]
            kj = ktab_ref[g * unroll + u]
            qt = qt_ref[0, qi]
            k0 = pl.multiple_of(kj * K_TILE, K_TILE)
            kt = jnp.concatenate([k_ref[0, pl.ds(k0, K_TILE), :], chunk_onehot], axis=1)
            hidden = (qry_chunk + qi * chunks_per_q) < (kj * chunks_per_k + pen_row)
            pen = jnp.where(hidden & (pen_row < chunks_per_k), NEG_BIG, 0.0).astype(jnp.bfloat16)
            pen = jnp.concatenate([pen, zeros[:LANES - PEN_ROWS]], axis=0)
            w0 = jnp.concatenate([qt[:HEAD_DIM], zeros[:HEAD_DIM], pen], axis=0)
            w1 = jnp.concatenate([zeros[:HEAD_DIM], qt[HEAD_DIM:], pen], axis=0)
            s_ref[slot, u, 0] = jnp.dot(kt, w0, preferred_element_type=jnp.float32)
            s_ref[slot, u, 1] = jnp.dot(kt, w1, preferred_element_type=jnp.float32)

    def softmax_pv(g, slot):
        for u in range(unroll):
            qi = qtab_ref[g * unroll + u]
            kj = ktab_ref[g * unroll + u]
            vt1 = jnp.concatenate([vt_ref[0, kj], ones_rows], axis=0)
            for c in range(2):
                s = s_ref[slot, u, c]
                m = m_ref[qi, c]
                m_new = jnp.maximum(m, jnp.max(s, axis=0, keepdims=True))
                p = jnp.exp2(s - m_new)
                alpha = jnp.exp2(m - m_new)
                pv = jnp.dot(vt1, p.astype(jnp.bfloat16), preferred_element_type=jnp.float32)
                m_ref[qi, c] = m_new
                l_ref[qi, c] = alpha * l_ref[qi, c] + pv[V_DIM:V_DIM + 1]
                acc_ref[qi, c] = alpha * acc_ref[qi, c] + pv[:V_DIM]

    scores(0, 0)

    def group_pair(j, carry):
        scores(2 * j + 1, 1)
        softmax_pv(2 * j, 0)
        scores(jnp.minimum(2 * j + 2, n_groups - 1), 0)
        softmax_pv(2 * j + 1, 1)
        return carry

    lax.fori_loop(0, n_groups // 2, group_pair, 0)

    def finish(qi, carry):
        o = acc_ref[qi, 0] * (1.0 / l_ref[qi, 0]) - acc_ref[qi, 1] * (lam / l_ref[qi, 1])
        y = o * lax.rsqrt(jnp.mean(o * o, axis=0, keepdims=True) + LN_EPS)
        q0 = pl.multiple_of(qi * Q_TILE, Q_TILE)
        o_ref[0, pl.ds(q0, Q_TILE), :] = y.T * (g_ref[...] * (1.0 - LAMBDA_INIT))
        return carry

    lax.fori_loop(0, nq, finish, 0, unroll=4)


def _attn_call(lam_params, subln_g, qt, k, vt):
    B, S, _ = k.shape
    nq = S // Q_TILE
    qtab, ktab = _attn_schedule(nq)
    n_steps = len(qtab)
    unroll = max(u for u in range(1, KV_UNROLL + 1) if n_steps % (2 * u) == 0)
    const = lambda b, h, *_: (0, 0)
    grid_spec = pltpu.PrefetchScalarGridSpec(
        num_scalar_prefetch=2,
        grid=(B, N_HEADS),
        in_specs=[pl.BlockSpec((4, HEAD_DIM), const),
                  pl.BlockSpec((1, V_DIM), const),
                  pl.BlockSpec((1, nq, V_DIM, Q_TILE), lambda b, h, *_: (b, 0, h, 0)),
                  pl.BlockSpec((1, S, LANES), lambda b, h, *_: (b, 0, h)),
                  pl.BlockSpec((1, S // K_TILE, V_DIM, K_TILE), lambda b, h, *_: (b, 0, h, 0))],
        out_specs=pl.BlockSpec((1, S, LANES), lambda b, h, *_: (b, 0, h)),
        scratch_shapes=[pltpu.VMEM((2, unroll, 2, K_TILE, Q_TILE), jnp.float32),
                        pltpu.VMEM((nq, 2, 1, Q_TILE), jnp.float32),
                        pltpu.VMEM((nq, 2, 1, Q_TILE), jnp.float32),
                        pltpu.VMEM((nq, 2, V_DIM, Q_TILE), jnp.float32)])
    return pl.pallas_call(
        functools.partial(_attn_kernel, n_steps=n_steps, unroll=unroll),
        grid_spec=grid_spec,
        out_shape=jax.ShapeDtypeStruct((B, S, D_MODEL), jnp.float32),
        compiler_params=pltpu.CompilerParams(
            dimension_semantics=("arbitrary", "arbitrary"), vmem_limit_bytes=VMEM_LIMIT),
        name="diff_attn",
    )(jnp.asarray(qtab), jnp.asarray(ktab), lam_params, subln_g, qt, k, vt)


def _depthwise_conv(ext_ref, dwk_ref, dwb_ref, out_ref, ts):
    base = CONV_HALO - (CONV_WIDTH - 1)
    groups = [[(j, (base + j) // 8) for j in range(CONV_WIDTH) if (base + j) % 8 == r]
              for r in range(8)]
    max_a = max(a for g in groups for _, a in g)
    rows = CONV_ROWS + CONV_PAD
    assert rows + 8 * max_a <= CONV_ROWS + CONV_HALO + CONV_PAD

    def col_body(c, carry):
        lanes = pl.ds(pl.multiple_of(c * LANES, LANES), LANES)
        kcol = dwk_ref[:, lanes]
        bias = dwb_ref[:, lanes]
        for rb in range(ts // CONV_ROWS):
            t0 = rb * CONV_ROWS
            e = ext_ref[t0:t0 + rows + 8 * max_a, lanes]
            w = None
            for r in range(7, -1, -1):
                v = None
                for j, a in groups[r]:
                    term = e[8 * a:8 * a + rows] * kcol[j:j + 1]
                    v = term if v is None else v + term
                if w is not None:
                    v = v + pltpu.roll(w, rows - 1, 0)
                w = v
            out_ref[t0:t0 + CONV_ROWS, lanes] = w[:CONV_ROWS] + bias
        return carry

    lax.fori_loop(0, D_MODEL // LANES, col_body, 0)


def _conv_kernel(x_ref, wg_ref, bg_ref, dwk_ref, dwb_ref, lng_ref, lnb_ref,
                 wp_ref, bp_ref, o_ref, ext_ref, acc_ref):
    ts = x_ref.shape[1]

    @pl.when(pl.program_id(1) == 0)
    def _():
        ext_ref[0:CONV_HALO, :] = jnp.zeros((CONV_HALO, D_MODEL), jnp.float32)

    xb = x_ref[0].astype(jnp.bfloat16)
    glu = jnp.dot(xb, wg_ref[...], preferred_element_type=jnp.float32) + bg_ref[...]
    ext_ref[CONV_HALO:CONV_HALO + ts, :] = glu[:, :D_MODEL] * jax.nn.sigmoid(glu[:, D_MODEL:])
    ext_ref[CONV_HALO + ts:, :] = jnp.zeros((CONV_PAD, D_MODEL), jnp.float32)
    _depthwise_conv(ext_ref, dwk_ref, dwb_ref, acc_ref, ts)
    ext_ref[0:CONV_HALO, :] = ext_ref[ts:ts + CONV_HALO, :]
    y = _layer_norm(acc_ref[...], lng_ref[...], lnb_ref[...])
    y = y * jax.nn.sigmoid(y)
    o_ref[0] = (jnp.dot(y.astype(jnp.bfloat16), wp_ref[...],
                        preferred_element_type=jnp.float32) + bp_ref[...])


def _conv_call(x, w_glu, b_glu, dw_kernel, dw_bias, ln_g, ln_b, w_pw2, b_pw2):
    B, S, _ = x.shape
    ts = ROW_TILE
    row = lambda b, i: (b, i, 0)
    const = lambda b, i: (0, 0)
    vec = pl.BlockSpec((1, D_MODEL), const)
    return pl.pallas_call(
        _conv_kernel,
        grid=(B, S // ts),
        in_specs=[pl.BlockSpec((1, ts, D_MODEL), row),
                  pl.BlockSpec((D_MODEL, 2 * D_MODEL), const),
                  pl.BlockSpec((1, 2 * D_MODEL), const),
                  pl.BlockSpec((CONV_WIDTH, D_MODEL), const),
                  vec, vec, vec,
                  pl.BlockSpec((D_MODEL, D_MODEL), const),
                  vec],
        out_specs=pl.BlockSpec((1, ts, D_MODEL), row),
        out_shape=jax.ShapeDtypeStruct((B, S, D_MODEL), jnp.float32),
        scratch_shapes=[pltpu.VMEM((ROW_TILE + CONV_HALO + CONV_PAD, D_MODEL), jnp.float32),
                        pltpu.VMEM((ROW_TILE, D_MODEL), jnp.float32)],
        compiler_params=pltpu.CompilerParams(
            dimension_semantics=("arbitrary", "arbitrary"), vmem_limit_bytes=VMEM_LIMIT),
        name="conv_branch",
    )(x, w_glu, b_glu, dw_kernel, dw_bias, ln_g, ln_b, w_pw2, b_pw2)


def _merge_kernel(x_ref, att_ref, conv_ref, wgate_ref, bgate_ref, wout_ref,
                  lng_ref, lnb_ref, o_ref):
    x = x_ref[...]
    gates = jnp.dot(x.astype(jnp.bfloat16), wgate_ref[...],
                    preferred_element_type=jnp.float32) + bgate_ref[...]
    g = jax.nn.sigmoid(gates)
    mix = g[:, :D_MODEL] * att_ref[...] + g[:, D_MODEL:] * conv_ref[...]
    mixed = jnp.dot(mix.astype(jnp.bfloat16), wout_ref[...],
                    preferred_element_type=jnp.float32)
    o_ref[...] = _layer_norm(DEEPNORM_ALPHA * x + mixed, lng_ref[...], lnb_ref[...])


def _merge_call(x2, att2, conv2, w_gate, b_gate, w_out, ln_g, ln_b):
    T = x2.shape[0]
    tm = ROW_TILE
    row = lambda i: (i, 0)
    const = lambda i: (0, 0)
    tile = pl.BlockSpec((tm, D_MODEL), row)
    vec = pl.BlockSpec((1, D_MODEL), const)
    return pl.pallas_call(
        _merge_kernel,
        grid=(T // tm,),
        in_specs=[tile, tile, tile,
                  pl.BlockSpec((D_MODEL, 2 * D_MODEL), const),
                  pl.BlockSpec((1, 2 * D_MODEL), const),
                  pl.BlockSpec((D_MODEL, D_MODEL), const),
                  vec, vec],
        out_specs=tile,
        out_shape=jax.ShapeDtypeStruct((T, D_MODEL), jnp.float32),
        compiler_params=pltpu.CompilerParams(
            dimension_semantics=("arbitrary",), vmem_limit_bytes=VMEM_LIMIT),
        name="gated_merge",
    )(x2, att2, conv2, w_gate, b_gate, w_out, ln_g, ln_b)


def _mlp_kernel(h_ref, w1_ref, w2_ref, lng_ref, lnb_ref, o_ref):
    h = h_ref[...]
    hb = h.astype(jnp.bfloat16)
    ff = None
    for c in range(D_FF // D_MODEL):
        a = jnp.dot(hb, w1_ref[:, c * D_MODEL:(c + 1) * D_MODEL],
                    preferred_element_type=jnp.float32)
        a = jnp.maximum(a, 0.0)
        a = (a * a).astype(jnp.bfloat16)
        part = jnp.dot(a, w2_ref[c * D_MODEL:(c + 1) * D_MODEL, :],
                       preferred_element_type=jnp.float32)
        ff = part if ff is None else ff + part
    o_ref[...] = _layer_norm(DEEPNORM_ALPHA * h + ff, lng_ref[...], lnb_ref[...])


def _mlp_call(h1, w_ff1, w_ff2, ln_g, ln_b):
    T = h1.shape[0]
    tm = ROW_TILE
    row = lambda i: (i, 0)
    const = lambda i: (0, 0)
    tile = pl.BlockSpec((tm, D_MODEL), row)
    vec = pl.BlockSpec((1, D_MODEL), const)
    return pl.pallas_call(
        _mlp_kernel,
        grid=(T // tm,),
        in_specs=[tile,
                  pl.BlockSpec((D_MODEL, D_FF), const),
                  pl.BlockSpec((D_FF, D_MODEL), const),
                  vec, vec],
        out_specs=tile,
        out_shape=jax.ShapeDtypeStruct((T, D_MODEL), jnp.float32),
        compiler_params=pltpu.CompilerParams(
            dimension_semantics=("arbitrary",), vmem_limit_bytes=VMEM_LIMIT),
        name="mlp_relu2",
    )(h1, w_ff1, w_ff2, ln_g, ln_b)


def kernel(x, positions, w_in, b_glu, b_gate, lambda_q1, lambda_k1, lambda_q2, lambda_k2, subln_g, dw_kernel, dw_bias, conv_ln_g, conv_ln_b, w_pw2, b_pw2, w_out, ln1_g, ln1_b, w_ff1, w_ff2, ln2_g, ln2_b):
    B, S, D = x.shape
    T = B * S
    bf16 = jnp.bfloat16
    inv_freq = ROPE_THETA ** (-jnp.arange(HALF, dtype=jnp.float32) * 2.0 / HEAD_DIM)
    inv_col = inv_freq[:, None]
    pos3 = positions.astype(jnp.float32).reshape(B, 1, S)
    x2 = x.reshape(T, D)

    l = 0
    w = w_in[l].astype(bf16)
    wt_qkv = w[:, :3 * D].T
    w_glu = w[:, 3 * D:5 * D]
    w_gate = w[:, 5 * D:]
    row = lambda a: a[l][None, :]

    qt, k, vt = _qkv_call(x, pos3, inv_col, wt_qkv)
    lam_params = jnp.stack([lambda_q1[l], lambda_k1[l], lambda_q2[l], lambda_k2[l]])
    att = _attn_call(lam_params, row(subln_g), qt, k, vt)
    conv = _conv_call(x, w_glu, row(b_glu), dw_kernel[l], row(dw_bias),
                      row(conv_ln_g), row(conv_ln_b), w_pw2[l].astype(bf16), row(b_pw2))
    h1 = _merge_call(x2, att.reshape(T, D), conv.reshape(T, D), w_gate, row(b_gate),
                     w_out[l].astype(bf16), row(ln1_g), row(ln1_b))
    out = _mlp_call(h1, w_ff1[l].astype(bf16), w_ff2[l].astype(bf16),
                    row(ln2_g), row(ln2_b))
    return out.reshape(B, S, D)
```

```python
import functools
import math

import jax
import jax.numpy as jnp
import numpy as np
from jax import lax
from jax.experimental import pallas as pl
from jax.experimental.pallas import tpu as pltpu

D_MODEL = 1024
N_HEADS = 8
HEAD_DIM = 64
HALF = HEAD_DIM // 2
V_DIM = 2 * HEAD_DIM
CHUNK = 64
CONV_WIDTH = 31
D_FF = 4 * D_MODEL
ROPE_THETA = 10000.0
LN_EPS = 1e-5
DEPTH = 1
DEEPNORM_ALPHA = (2.0 * DEPTH) ** 0.25
LAMBDA_INIT = 0.8 - 0.6 * math.exp(-0.3 * 0)
GLU_COL0 = 3 * D_MODEL
GATE_COL0 = 5 * D_MODEL

LANES = 128
VMEM_LIMIT = 56 * 1024 * 1024

ROW_TILE = 512
Q_TILE = 512
K_TILE = 512
KV_UNROLL = 3
ONES_ROWS = 16
PEN_ROWS = 16
CONV_HALO = 32
CONV_PAD = 8
CONV_ROWS = 64
NEG_BIG = -1e30
LOG2_E = math.log2(math.e)

_NT = (((1,), (1,)), ((), ()))


def _layer_norm(y, g, b):
    mu = jnp.mean(y, axis=-1, keepdims=True)
    d = y - mu
    var = jnp.mean(d * d, axis=-1, keepdims=True)
    return d * lax.rsqrt(var + LN_EPS) * g + b


def _rope_rows(t, cos, sin):
    pieces = []
    for g in range(D_MODEL // HEAD_DIM):
        t1 = t[g * HEAD_DIM:g * HEAD_DIM + HALF]
        t2 = t[g * HEAD_DIM + HALF:(g + 1) * HEAD_DIM]
        pieces.append(t1 * cos - t2 * sin)
        pieces.append(t1 * sin + t2 * cos)
    return jnp.concatenate(pieces, axis=0)


def _qkv_kernel(x_ref, pos_ref, inv_ref, wt_ref, qt_ref, k_ref, vt_ref):
    tm = x_ref.shape[1]
    xb = x_ref[0].astype(jnp.bfloat16)
    ang = inv_ref[...] * pos_ref[0]
    cos = jnp.cos(ang)
    sin = jnp.sin(ang)
    scale = HEAD_DIM ** -0.5 * LOG2_E

    qt = lax.dot_general(wt_ref[0:D_MODEL, :], xb, _NT,
                         preferred_element_type=jnp.float32)
    qt = _rope_rows(qt, cos * scale, sin * scale).astype(qt_ref.dtype)
    for j in range(tm // Q_TILE):
        qt_ref[0, j] = qt[:, j * Q_TILE:(j + 1) * Q_TILE]

    kt = lax.dot_general(wt_ref[D_MODEL:2 * D_MODEL, :], xb, _NT,
                         preferred_element_type=jnp.float32)
    k_ref[0] = _rope_rows(kt, cos, sin).T.astype(k_ref.dtype)

    vt = lax.dot_general(wt_ref[2 * D_MODEL:3 * D_MODEL, :], xb, _NT,
                         preferred_element_type=jnp.float32).astype(vt_ref.dtype)
    for j in range(tm // K_TILE):
        vt_ref[0, j] = vt[:, j * K_TILE:(j + 1) * K_TILE]


def _qkv_call(x, pos3, inv_col, wt_qkv):
    B, S, _ = x.shape
    tm = ROW_TILE
    const = lambda b, i: (0, 0)
    return pl.pallas_call(
        _qkv_kernel,
        grid=(B, S // tm),
        in_specs=[pl.BlockSpec((1, tm, D_MODEL), lambda b, i: (b, i, 0)),
                  pl.BlockSpec((1, 1, tm), lambda b, i: (b, 0, i)),
                  pl.BlockSpec((HALF, 1), const),
                  pl.BlockSpec((3 * D_MODEL, D_MODEL), const)],
        out_specs=[pl.BlockSpec((1, tm // Q_TILE, D_MODEL, Q_TILE), lambda b, i: (b, i, 0, 0)),
                   pl.BlockSpec((1, tm, D_MODEL), lambda b, i: (b, i, 0)),
                   pl.BlockSpec((1, tm // K_TILE, D_MODEL, K_TILE), lambda b, i: (b, i, 0, 0))],
        out_shape=[jax.ShapeDtypeStruct((B, S // Q_TILE, D_MODEL, Q_TILE), jnp.bfloat16),
                   jax.ShapeDtypeStruct((B, S, D_MODEL), jnp.bfloat16),
                   jax.ShapeDtypeStruct((B, S // K_TILE, D_MODEL, K_TILE), jnp.bfloat16)],
        compiler_params=pltpu.CompilerParams(
            dimension_semantics=("arbitrary", "arbitrary"), vmem_limit_bytes=VMEM_LIMIT),
        name="qkv_rope",
    )(x, pos3, inv_col, wt_qkv)


def _attn_schedule(nq):
    tiles_per_q = Q_TILE // K_TILE
    qs, ks = [], []
    for qi in range(nq):
        for kj in range((qi + 1) * tiles_per_q):
            qs.append(qi)
            ks.append(kj)
    return np.asarray(qs, np.int32), np.asarray(ks, np.int32)


def _attn_kernel(qtab_ref, ktab_ref, lam_ref, g_ref, qt_ref, k_ref, vt_ref, o_ref,
                 s_ref, m_ref, l_ref, acc_ref, *, n_steps, unroll):
    nq = qt_ref.shape[1]
    n_groups = n_steps // unroll
    chunks_per_k = K_TILE // CHUNK
    chunks_per_q = Q_TILE // CHUNK
    lp = lam_ref[...]
    lam = (jnp.exp(jnp.sum(lp[0:1] * lp[1:2], axis=-1, keepdims=True))
           - jnp.exp(jnp.sum(lp[2:3] * lp[3:4], axis=-1, keepdims=True))
           + LAMBDA_INIT)
    qry_chunk = lax.broadcasted_iota(jnp.int32, (1, Q_TILE), 1) // CHUNK
    pen_row = lax.broadcasted_iota(jnp.int32, (PEN_ROWS, Q_TILE), 0)
    key_row_chunk = lax.broadcasted_iota(jnp.int32, (K_TILE, LANES), 0) // CHUNK
    key_lane = lax.broadcasted_iota(jnp.int32, (K_TILE, LANES), 1)
    chunk_onehot = jnp.where(key_row_chunk == key_lane, 1.0, 0.0).astype(jnp.bfloat16)
    ones_rows = jnp.ones((ONES_ROWS, K_TILE), jnp.bfloat16)
    zeros = jnp.zeros((LANES, Q_TILE), jnp.bfloat16)

    m_ref[...] = jnp.full_like(m_ref, NEG_BIG)
    l_ref[...] = jnp.zeros_like(l_ref)
    acc_ref[...] = jnp.zeros_like(acc_ref)

    def scores(g, slot):
        for u in range(unroll):
            qi = qtab_ref[g * unroll + u]
            kj = ktab_ref[g * unroll + u]
            qt = qt_ref[0, qi]
            k0 = pl.multiple_of(kj * K_TILE, K_TILE)
            kt = jnp.concatenate([k_ref[0, pl.ds(k0, K_TILE), :], chunk_onehot], axis=1)
            hidden = (qry_chunk + qi * chunks_per_q) < (kj * chunks_per_k + pen_row)
            pen = jnp.where(hidden & (pen_row < chunks_per_k), NEG_BIG, 0.0).astype(jnp.bfloat16)
            pen = jnp.concatenate([pen, zeros[:LANES - PEN_ROWS]], axis=0)
            w0 = jnp.concatenate([qt[:HEAD_DIM], zeros[:HEAD_DIM], pen], axis=0)
            w1 = jnp.concatenate([zeros[:HEAD_DIM], qt[HEAD_DIM:], pen], axis=0)
            s_ref[slot, u, 0] = jnp.dot(kt, w0, preferred_element_type=jnp.float32)
            s_ref[slot, u, 1] = jnp.dot(kt, w1, preferred_element_type=jnp.float32)

    def softmax_pv(g, slot):
        for u in range(unroll):
            qi = qtab_ref[g * unroll + u]
            kj = ktab_ref[g * unroll + u]
            vt1 = jnp.concatenate([vt_ref[0, kj], ones_rows], axis=0)
            for c in range(2):
                s = s_ref[slot, u, c]
                m = m_ref[qi, c]
                m_new = jnp.maximum(m, jnp.max(s, axis=0, keepdims=True))
                p = jnp.exp2(s - m_new)
                alpha = jnp.exp2(m - m_new)
                pv = jnp.dot(vt1, p.astype(jnp.bfloat16), preferred_element_type=jnp.float32)
                m_ref[qi, c] = m_new
                l_ref[qi, c] = alpha * l_ref[qi, c] + pv[V_DIM:V_DIM + 1]
                acc_ref[qi, c] = alpha * acc_ref[qi, c] + pv[:V_DIM]

    scores(0, 0)

    def group_pair(j, carry):
        scores(2 * j + 1, 1)
        softmax_pv(2 * j, 0)
        scores(jnp.minimum(2 * j + 2, n_groups - 1), 0)
        softmax_pv(2 * j + 1, 1)
        return carry

    lax.fori_loop(0, n_groups // 2, group_pair, 0)

    def finish(qi, carry):
        o = acc_ref[qi, 0] * (1.0 / l_ref[qi, 0]) - acc_ref[qi, 1] * (lam / l_ref[qi, 1])
        y = o * lax.rsqrt(jnp.mean(o * o, axis=0, keepdims=True) + LN_EPS)
        q0 = pl.multiple_of(qi * Q_TILE, Q_TILE)
        o_ref[0, pl.ds(q0, Q_TILE), :] = y.T * (g_ref[...] * (1.0 - LAMBDA_INIT))
        return carry

    lax.fori_loop(0, nq, finish, 0, unroll=4)


def _attn_call(lam_params, subln_g, qt, k, vt):
    B, S, _ = k.shape
    nq = S // Q_TILE
    qtab, ktab = _attn_schedule(nq)
    n_steps = len(qtab)
    unroll = max(u for u in range(1, KV_UNROLL + 1) if n_steps % (2 * u) == 0)
    const = lambda b, h, *_: (0, 0)
    grid_spec = pltpu.PrefetchScalarGridSpec(
        num_scalar_prefetch=2,
        grid=(B, N_HEADS),
        in_specs=[pl.BlockSpec((4, HEAD_DIM), const),
                  pl.BlockSpec((1, V_DIM), const),
                  pl.BlockSpec((1, nq, V_DIM, Q_TILE), lambda b, h, *_: (b, 0, h, 0)),
                  pl.BlockSpec((1, S, LANES), lambda b, h, *_: (b, 0, h)),
                  pl.BlockSpec((1, S // K_TILE, V_DIM, K_TILE), lambda b, h, *_: (b, 0, h, 0))],
        out_specs=pl.BlockSpec((1, S, LANES), lambda b, h, *_: (b, 0, h)),
        scratch_shapes=[pltpu.VMEM((2, unroll, 2, K_TILE, Q_TILE), jnp.float32),
                        pltpu.VMEM((nq, 2, 1, Q_TILE), jnp.float32),
                        pltpu.VMEM((nq, 2, 1, Q_TILE), jnp.float32),
                        pltpu.VMEM((nq, 2, V_DIM, Q_TILE), jnp.float32)])
    return pl.pallas_call(
        functools.partial(_attn_kernel, n_steps=n_steps, unroll=unroll),
        grid_spec=grid_spec,
        out_shape=jax.ShapeDtypeStruct((B, S, D_MODEL), jnp.float32),
        compiler_params=pltpu.CompilerParams(
            dimension_semantics=("arbitrary", "arbitrary"), vmem_limit_bytes=VMEM_LIMIT),
        name="diff_attn",
    )(jnp.asarray(qtab), jnp.asarray(ktab), lam_params, subln_g, qt, k, vt)


def _depthwise_conv(ext_ref, dwk_ref, dwb_ref, out_ref, ts):
    base = CONV_HALO - (CONV_WIDTH - 1)
    groups = [[(j, (base + j) // 8) for j in range(CONV_WIDTH) if (base + j) % 8 == r]
              for r in range(8)]
    max_a = max(a for g in groups for _, a in g)
    rows = CONV_ROWS + CONV_PAD
    assert rows + 8 * max_a <= CONV_ROWS + CONV_HALO + CONV_PAD

    def col_body(c, carry):
        lanes = pl.ds(pl.multiple_of(c * LANES, LANES), LANES)
        kcol = dwk_ref[:, lanes]
        bias = dwb_ref[:, lanes]
        for rb in range(ts // CONV_ROWS):
            t0 = rb * CONV_ROWS
            e = ext_ref[t0:t0 + rows + 8 * max_a, lanes]
            w = None
            for r in range(7, -1, -1):
                v = None
                for j, a in groups[r]:
                    term = e[8 * a:8 * a + rows] * kcol[j:j + 1]
                    v = term if v is None else v + term
                if w is not None:
                    v = v + pltpu.roll(w, rows - 1, 0)
                w = v
            out_ref[t0:t0 + CONV_ROWS, lanes] = w[:CONV_ROWS] + bias
        return carry

    lax.fori_loop(0, D_MODEL // LANES, col_body, 0)


def _conv_kernel(x_ref, wga_ref, wgb_ref, bg_ref, dwk_ref, dwb_ref, lng_ref, lnb_ref,
                 wp_ref, bp_ref, o_ref, ext_ref, acc_ref):
    ts = x_ref.shape[1]

    @pl.when(pl.program_id(1) == 0)
    def _():
        ext_ref[0:CONV_HALO, :] = jnp.zeros((CONV_HALO, D_MODEL), jnp.float32)

    xb = x_ref[0].astype(jnp.bfloat16)
    ga = jnp.dot(xb, wga_ref[...], preferred_element_type=jnp.float32) + bg_ref[:, :D_MODEL]
    gb = jnp.dot(xb, wgb_ref[...], preferred_element_type=jnp.float32) + bg_ref[:, D_MODEL:]
    ext_ref[CONV_HALO:CONV_HALO + ts, :] = ga * jax.nn.sigmoid(gb)
    ext_ref[CONV_HALO + ts:, :] = jnp.zeros((CONV_PAD, D_MODEL), jnp.float32)
    _depthwise_conv(ext_ref, dwk_ref, dwb_ref, acc_ref, ts)
    ext_ref[0:CONV_HALO, :] = ext_ref[ts:ts + CONV_HALO, :]
    y = _layer_norm(acc_ref[...], lng_ref[...], lnb_ref[...])
    y = y * jax.nn.sigmoid(y)
    o_ref[0] = (jnp.dot(y.astype(jnp.bfloat16), wp_ref[...],
                        preferred_element_type=jnp.float32) + bp_ref[...])


def _conv_call(x, w_in_bf16, b_glu, dw_kernel, dw_bias, ln_g, ln_b, w_pw2, b_pw2):
    B, S, _ = x.shape
    ts = ROW_TILE
    row = lambda b, i: (b, i, 0)
    const = lambda b, i: (0, 0)
    vec = pl.BlockSpec((1, D_MODEL), const)
    glu_col = GLU_COL0 // D_MODEL
    return pl.pallas_call(
        _conv_kernel,
        grid=(B, S // ts),
        in_specs=[pl.BlockSpec((1, ts, D_MODEL), row),
                  pl.BlockSpec((D_MODEL, D_MODEL), lambda b, i: (0, glu_col)),
                  pl.BlockSpec((D_MODEL, D_MODEL), lambda b, i: (0, glu_col + 1)),
                  pl.BlockSpec((1, 2 * D_MODEL), const),
                  pl.BlockSpec((CONV_WIDTH, D_MODEL), const),
                  vec, vec, vec,
                  pl.BlockSpec((D_MODEL, D_MODEL), const),
                  vec],
        out_specs=pl.BlockSpec((1, ts, D_MODEL), row),
        out_shape=jax.ShapeDtypeStruct((B, S, D_MODEL), jnp.float32),
        scratch_shapes=[pltpu.VMEM((ROW_TILE + CONV_HALO + CONV_PAD, D_MODEL), jnp.float32),
                        pltpu.VMEM((ROW_TILE, D_MODEL), jnp.float32)],
        compiler_params=pltpu.CompilerParams(
            dimension_semantics=("arbitrary", "arbitrary"), vmem_limit_bytes=VMEM_LIMIT),
        name="conv_branch",
    )(x, w_in_bf16, w_in_bf16, b_glu, dw_kernel, dw_bias, ln_g, ln_b, w_pw2, b_pw2)


def _merge_mlp_kernel(x_ref, att_ref, conv_ref, wga_ref, wgc_ref, bgate_ref, wout_ref,
                      l1g_ref, l1b_ref, w1_ref, w2_ref, l2g_ref, l2b_ref, o_ref):
    half_rows = ROW_TILE // 2

    def merge(rows):
        x = x_ref[rows, :]
        xb = x.astype(jnp.bfloat16)
        g_att = jax.nn.sigmoid(jnp.dot(xb, wga_ref[...], preferred_element_type=jnp.float32)
                               + bgate_ref[:, :D_MODEL])
        g_conv = jax.nn.sigmoid(jnp.dot(xb, wgc_ref[...], preferred_element_type=jnp.float32)
                                + bgate_ref[:, D_MODEL:])
        mix = g_att * att_ref[rows, :] + g_conv * conv_ref[rows, :]
        mixed = jnp.dot(mix.astype(jnp.bfloat16), wout_ref[...], preferred_element_type=jnp.float32)
        return _layer_norm(DEEPNORM_ALPHA * x + mixed, l1g_ref[...], l1b_ref[...])

    def mlp(h1):
        hb = h1.astype(jnp.bfloat16)
        ff = None
        for c in range(D_FF // D_MODEL):
            a = jnp.dot(hb, w1_ref[:, c * D_MODEL:(c + 1) * D_MODEL],
                        preferred_element_type=jnp.float32)
            a = jnp.maximum(a, 0.0)
            a = (a * a).astype(jnp.bfloat16)
            part = jnp.dot(a, w2_ref[c * D_MODEL:(c + 1) * D_MODEL, :],
                           preferred_element_type=jnp.float32)
            ff = part if ff is None else ff + part
        return _layer_norm(DEEPNORM_ALPHA * h1 + ff, l2g_ref[...], l2b_ref[...])

    halves = [slice(h * half_rows, (h + 1) * half_rows) for h in range(2)]
    h1s = [merge(rows) for rows in halves]
    for rows, h1 in zip(halves, h1s):
        o_ref[rows, :] = mlp(h1)


def _merge_mlp_call(x2, att2, conv2, w_in_bf16, b_gate, w_out, ln1_g, ln1_b, w_ff1, w_ff2, ln2_g, ln2_b):
    T = x2.shape[0]
    tm = ROW_TILE
    row = lambda i: (i, 0)
    const = lambda i: (0, 0)
    wspec = lambda shape, index_map=const: pl.BlockSpec(shape, index_map, pipeline_mode=pl.Buffered(1))
    tile = pl.BlockSpec((tm, D_MODEL), row)
    vec = wspec((1, D_MODEL))
    gate_col = GATE_COL0 // D_MODEL
    return pl.pallas_call(
        _merge_mlp_kernel,
        grid=(T // tm,),
        in_specs=[tile, tile, tile,
                  wspec((D_MODEL, D_MODEL), lambda i: (0, gate_col)),
                  wspec((D_MODEL, D_MODEL), lambda i: (0, gate_col + 1)),
                  wspec((1, 2 * D_MODEL)),
                  wspec((D_MODEL, D_MODEL)), vec, vec,
                  wspec((D_MODEL, D_FF)), wspec((D_FF, D_MODEL)), vec, vec],
        out_specs=tile,
        out_shape=jax.ShapeDtypeStruct((T, D_MODEL), jnp.float32),
        compiler_params=pltpu.CompilerParams(
            dimension_semantics=("arbitrary",), vmem_limit_bytes=VMEM_LIMIT),
        name="merge_mlp",
    )(x2, att2, conv2, w_in_bf16, w_in_bf16, b_gate, w_out, ln1_g, ln1_b, w_ff1, w_ff2, ln2_g, ln2_b)


def kernel(x, positions, w_in, b_glu, b_gate, lambda_q1, lambda_k1, lambda_q2, lambda_k2, subln_g, dw_kernel, dw_bias, conv_ln_g, conv_ln_b, w_pw2, b_pw2, w_out, ln1_g, ln1_b, w_ff1, w_ff2, ln2_g, ln2_b):
    B, S, D = x.shape
    T = B * S
    bf16 = jnp.bfloat16
    inv_freq = ROPE_THETA ** (-jnp.arange(HALF, dtype=jnp.float32) * 2.0 / HEAD_DIM)
    inv_col = inv_freq[:, None]
    pos3 = positions.astype(jnp.float32).reshape(B, 1, S)
    x2 = x.reshape(T, D)

    l = 0
    w = w_in[l].astype(bf16)
    wt_qkv = w[:, :GLU_COL0].T
    row = lambda a: a[l][None, :]

    qt, k, vt = _qkv_call(x, pos3, inv_col, wt_qkv)
    lam_params = jnp.stack([lambda_q1[l], lambda_k1[l], lambda_q2[l], lambda_k2[l]])
    att = _attn_call(lam_params, row(subln_g), qt, k, vt)
    conv = _conv_call(x, w, row(b_glu), dw_kernel[l], row(dw_bias),
                      row(conv_ln_g), row(conv_ln_b), w_pw2[l].astype(bf16), row(b_pw2))
    out = _merge_mlp_call(x2, att.reshape(T, D), conv.reshape(T, D), w, row(b_gate),
                          w_out[l].astype(bf16), row(ln1_g), row(ln1_b),
                          w_ff1[l].astype(bf16), w_ff2[l].astype(bf16), row(ln2_g), row(ln2_b))
    return out.reshape(B, S, D)
```

```python
import functools
import math

import jax
import jax.numpy as jnp
import numpy as np
from jax import lax
from jax.experimental import pallas as pl
from jax.experimental.pallas import tpu as pltpu

D_MODEL = 1024
N_HEADS = 8
HEAD_DIM = 64
HALF = HEAD_DIM // 2
V_DIM = 2 * HEAD_DIM
CHUNK = 64
CONV_WIDTH = 31
D_FF = 4 * D_MODEL
ROPE_THETA = 10000.0
LN_EPS = 1e-5
DEPTH = 1
DEEPNORM_ALPHA = (2.0 * DEPTH) ** 0.25
LAMBDA_INIT = 0.8 - 0.6 * math.exp(-0.3 * 0)
GLU_COL0 = 3 * D_MODEL
GATE_COL0 = 5 * D_MODEL

LANES = 128
VMEM_LIMIT = 56 * 1024 * 1024

ROW_TILE = 512
Q_TILE = 512
K_TILE = 512
KV_UNROLL = 3
ONES_ROWS = 16
PEN_ROWS = 16
CONV_HALO = 32
CONV_PAD = 8
CONV_ROWS = 64
NEG_BIG = -1e30
LOG2_E = math.log2(math.e)

_NT = (((1,), (1,)), ((), ()))


def _layer_norm(y, g, b):
    mu = jnp.mean(y, axis=-1, keepdims=True)
    d = y - mu
    var = jnp.mean(d * d, axis=-1, keepdims=True)
    return d * lax.rsqrt(var + LN_EPS) * g + b


def _rope_rows(t, cos, sin):
    pieces = []
    for g in range(D_MODEL // HEAD_DIM):
        t1 = t[g * HEAD_DIM:g * HEAD_DIM + HALF]
        t2 = t[g * HEAD_DIM + HALF:(g + 1) * HEAD_DIM]
        pieces.append(t1 * cos - t2 * sin)
        pieces.append(t1 * sin + t2 * cos)
    return jnp.concatenate(pieces, axis=0)


def _qkv_kernel(x_ref, pos_ref, inv_ref, wt_ref, qt_ref, k_ref, vt_ref):
    tm = x_ref.shape[1]
    xb = x_ref[0].astype(jnp.bfloat16)
    ang = inv_ref[...] * pos_ref[0]
    cos = jnp.cos(ang)
    sin = jnp.sin(ang)
    scale = HEAD_DIM ** -0.5 * LOG2_E

    qt = lax.dot_general(wt_ref[0:D_MODEL, :], xb, _NT,
                         preferred_element_type=jnp.float32)
    qt = _rope_rows(qt, cos * scale, sin * scale).astype(qt_ref.dtype)
    for j in range(tm // Q_TILE):
        qt_ref[0, j] = qt[:, j * Q_TILE:(j + 1) * Q_TILE]

    kt = lax.dot_general(wt_ref[D_MODEL:2 * D_MODEL, :], xb, _NT,
                         preferred_element_type=jnp.float32)
    k_ref[0] = _rope_rows(kt, cos, sin).T.astype(k_ref.dtype)

    vt = lax.dot_general(wt_ref[2 * D_MODEL:3 * D_MODEL, :], xb, _NT,
                         preferred_element_type=jnp.float32).astype(vt_ref.dtype)
    for j in range(tm // K_TILE):
        vt_ref[0, j] = vt[:, j * K_TILE:(j + 1) * K_TILE]


def _qkv_call(x, pos3, inv_col, wt_qkv):
    B, S, _ = x.shape
    tm = ROW_TILE
    const = lambda b, i: (0, 0)
    return pl.pallas_call(
        _qkv_kernel,
        grid=(B, S // tm),
        in_specs=[pl.BlockSpec((1, tm, D_MODEL), lambda b, i: (b, i, 0)),
                  pl.BlockSpec((1, 1, tm), lambda b, i: (b, 0, i)),
                  pl.BlockSpec((HALF, 1), const),
                  pl.BlockSpec((3 * D_MODEL, D_MODEL), const)],
        out_specs=[pl.BlockSpec((1, tm // Q_TILE, D_MODEL, Q_TILE), lambda b, i: (b, i, 0, 0)),
                   pl.BlockSpec((1, tm, D_MODEL), lambda b, i: (b, i, 0)),
                   pl.BlockSpec((1, tm // K_TILE, D_MODEL, K_TILE), lambda b, i: (b, i, 0, 0))],
        out_shape=[jax.ShapeDtypeStruct((B, S // Q_TILE, D_MODEL, Q_TILE), jnp.bfloat16),
                   jax.ShapeDtypeStruct((B, S, D_MODEL), jnp.bfloat16),
                   jax.ShapeDtypeStruct((B, S // K_TILE, D_MODEL, K_TILE), jnp.bfloat16)],
        compiler_params=pltpu.CompilerParams(
            dimension_semantics=("arbitrary", "arbitrary"), vmem_limit_bytes=VMEM_LIMIT),
        name="qkv_rope",
    )(x, pos3, inv_col, wt_qkv)


def _attn_schedule(nq):
    tiles_per_q = Q_TILE // K_TILE
    qs, ks = [], []
    for qi in range(nq):
        for kj in range((qi + 1) * tiles_per_q):
            qs.append(qi)
            ks.append(kj)
    return np.asarray(qs, np.int32), np.asarray(ks, np.int32)


def _attn_kernel(qtab_ref, ktab_ref, lam_ref, g_ref, qt_ref, k_ref, vt_ref, o_ref,
                 s_ref, m_ref, l_ref, acc_ref, *, n_steps, unroll):
    nq = qt_ref.shape[1]
    n_groups = n_steps // unroll
    chunks_per_k = K_TILE // CHUNK
    chunks_per_q = Q_TILE // CHUNK
    lp = lam_ref[...]
    lam = (jnp.exp(jnp.sum(lp[0:1] * lp[1:2], axis=-1, keepdims=True))
           - jnp.exp(jnp.sum(lp[2:3] * lp[3:4], axis=-1, keepdims=True))
           + LAMBDA_INIT)
    qry_chunk = lax.broadcasted_iota(jnp.int32, (1, Q_TILE), 1) // CHUNK
    pen_row = lax.broadcasted_iota(jnp.int32, (PEN_ROWS, Q_TILE), 0)
    key_row_chunk = lax.broadcasted_iota(jnp.int32, (K_TILE, LANES), 0) // CHUNK
    key_lane = lax.broadcasted_iota(jnp.int32, (K_TILE, LANES), 1)
    comp0_lane = key_lane < HEAD_DIM
    chunk_onehot = jnp.where(key_row_chunk == key_lane % HEAD_DIM, 1.0, 0.0).astype(jnp.bfloat16)
    ones_rows = jnp.ones((ONES_ROWS, K_TILE), jnp.bfloat16)
    zeros = jnp.zeros((HEAD_DIM - PEN_ROWS, Q_TILE), jnp.bfloat16)

    m_ref[...] = jnp.full_like(m_ref, NEG_BIG)
    l_ref[...] = jnp.zeros_like(l_ref)
    acc_ref[...] = jnp.zeros_like(acc_ref)

    def scores(g, slot):
        for u in range(unroll):
            qi = qtab_ref[g * unroll + u]
            kj = ktab_ref[g * unroll + u]
            qt = qt_ref[0, qi]
            k0 = pl.multiple_of(kj * K_TILE, K_TILE)
            kt = k_ref[0, pl.ds(k0, K_TILE), :]
            kt0 = jnp.where(comp0_lane, kt, chunk_onehot)
            kt1 = jnp.where(comp0_lane, chunk_onehot, kt)
            hidden = (qry_chunk + qi * chunks_per_q) < (kj * chunks_per_k + pen_row)
            pen = jnp.where(hidden & (pen_row < chunks_per_k), NEG_BIG, 0.0).astype(jnp.bfloat16)
            w0 = jnp.concatenate([qt[:HEAD_DIM], pen, zeros], axis=0)
            w1 = jnp.concatenate([pen, zeros, qt[HEAD_DIM:]], axis=0)
            s_ref[slot, u, 0] = jnp.dot(kt0, w0, preferred_element_type=jnp.float32)
            s_ref[slot, u, 1] = jnp.dot(kt1, w1, preferred_element_type=jnp.float32)

    def softmax_pv(g, slot):
        for u in range(unroll):
            qi = qtab_ref[g * unroll + u]
            kj = ktab_ref[g * unroll + u]
            vt1 = jnp.concatenate([vt_ref[0, kj], ones_rows], axis=0)
            for c in range(2):
                s = s_ref[slot, u, c]
                m = m_ref[qi, c]
                m_new = jnp.maximum(m, jnp.max(s, axis=0, keepdims=True))
                p = jnp.exp2(s - m_new)
                alpha = jnp.exp2(m - m_new)
                pv = jnp.dot(vt1, p.astype(jnp.bfloat16), preferred_element_type=jnp.float32)
                m_ref[qi, c] = m_new
                l_ref[qi, c] = alpha * l_ref[qi, c] + pv[V_DIM:V_DIM + 1]
                acc_ref[qi, c] = alpha * acc_ref[qi, c] + pv[:V_DIM]

    scores(0, 0)

    def group_pair(j, carry):
        scores(2 * j + 1, 1)
        softmax_pv(2 * j, 0)
        scores(jnp.minimum(2 * j + 2, n_groups - 1), 0)
        softmax_pv(2 * j + 1, 1)
        return carry

    lax.fori_loop(0, n_groups // 2, group_pair, 0)

    def finish(qi, carry):
        o = acc_ref[qi, 0] * (1.0 / l_ref[qi, 0]) - acc_ref[qi, 1] * (lam / l_ref[qi, 1])
        y = o * lax.rsqrt(jnp.mean(o * o, axis=0, keepdims=True) + LN_EPS)
        q0 = pl.multiple_of(qi * Q_TILE, Q_TILE)
        o_ref[0, pl.ds(q0, Q_TILE), :] = y.T * (g_ref[...] * (1.0 - LAMBDA_INIT))
        return carry

    lax.fori_loop(0, nq, finish, 0, unroll=4)


def _attn_call(lam_params, subln_g, qt, k, vt):
    B, S, _ = k.shape
    nq = S // Q_TILE
    qtab, ktab = _attn_schedule(nq)
    n_steps = len(qtab)
    unroll = max(u for u in range(1, KV_UNROLL + 1) if n_steps % (2 * u) == 0)
    const = lambda b, h, *_: (0, 0)
    grid_spec = pltpu.PrefetchScalarGridSpec(
        num_scalar_prefetch=2,
        grid=(B, N_HEADS),
        in_specs=[pl.BlockSpec((4, HEAD_DIM), const),
                  pl.BlockSpec((1, V_DIM), const),
                  pl.BlockSpec((1, nq, V_DIM, Q_TILE), lambda b, h, *_: (b, 0, h, 0)),
                  pl.BlockSpec((1, S, LANES), lambda b, h, *_: (b, 0, h)),
                  pl.BlockSpec((1, S // K_TILE, V_DIM, K_TILE), lambda b, h, *_: (b, 0, h, 0))],
        out_specs=pl.BlockSpec((1, S, LANES), lambda b, h, *_: (b, 0, h)),
        scratch_shapes=[pltpu.VMEM((2, unroll, 2, K_TILE, Q_TILE), jnp.float32),
                        pltpu.VMEM((nq, 2, 1, Q_TILE), jnp.float32),
                        pltpu.VMEM((nq, 2, 1, Q_TILE), jnp.float32),
                        pltpu.VMEM((nq, 2, V_DIM, Q_TILE), jnp.float32)])
    return pl.pallas_call(
        functools.partial(_attn_kernel, n_steps=n_steps, unroll=unroll),
        grid_spec=grid_spec,
        out_shape=jax.ShapeDtypeStruct((B, S, D_MODEL), jnp.float32),
        compiler_params=pltpu.CompilerParams(
            dimension_semantics=("arbitrary", "arbitrary"), vmem_limit_bytes=VMEM_LIMIT),
        name="diff_attn",
    )(jnp.asarray(qtab), jnp.asarray(ktab), lam_params, subln_g, qt, k, vt)


def _depthwise_conv(ext_ref, dwk_ref, dwb_ref, out_ref, ts):
    base = CONV_HALO - (CONV_WIDTH - 1)
    groups = [[(j, (base + j) // 8) for j in range(CONV_WIDTH) if (base + j) % 8 == r]
              for r in range(8)]
    max_a = max(a for g in groups for _, a in g)
    rows = CONV_ROWS + CONV_PAD
    assert rows + 8 * max_a <= CONV_ROWS + CONV_HALO + CONV_PAD

    def col_body(c, carry):
        lanes = pl.ds(pl.multiple_of(c * LANES, LANES), LANES)
        kcol = dwk_ref[:, lanes]
        bias = dwb_ref[:, lanes]
        for rb in range(ts // CONV_ROWS):
            t0 = rb * CONV_ROWS
            e = ext_ref[t0:t0 + rows + 8 * max_a, lanes]
            w = None
            for r in range(7, -1, -1):
                v = None
                for j, a in groups[r]:
                    term = e[8 * a:8 * a + rows] * kcol[j:j + 1]
                    v = term if v is None else v + term
                if w is not None:
                    v = v + pltpu.roll(w, rows - 1, 0)
                w = v
            out_ref[t0:t0 + CONV_ROWS, lanes] = w[:CONV_ROWS] + bias
        return carry

    lax.fori_loop(0, D_MODEL // LANES, col_body, 0)


def _conv_kernel(x_ref, wga_ref, wgb_ref, bg_ref, dwk_ref, dwb_ref, lng_ref, lnb_ref,
                 wp_ref, bp_ref, o_ref, ext_ref, acc_ref):
    ts = x_ref.shape[1]

    @pl.when(pl.program_id(1) == 0)
    def _():
        ext_ref[0:CONV_HALO, :] = jnp.zeros((CONV_HALO, D_MODEL), jnp.float32)

    xb = x_ref[0].astype(jnp.bfloat16)
    ga = jnp.dot(xb, wga_ref[...], preferred_element_type=jnp.float32) + bg_ref[:, :D_MODEL]
    gb = jnp.dot(xb, wgb_ref[...], preferred_element_type=jnp.float32) + bg_ref[:, D_MODEL:]
    ext_ref[CONV_HALO:CONV_HALO + ts, :] = ga * jax.nn.sigmoid(gb)
    ext_ref[CONV_HALO + ts:, :] = jnp.zeros((CONV_PAD, D_MODEL), jnp.float32)
    _depthwise_conv(ext_ref, dwk_ref, dwb_ref, acc_ref, ts)
    ext_ref[0:CONV_HALO, :] = ext_ref[ts:ts + CONV_HALO, :]
    y = _layer_norm(acc_ref[...], lng_ref[...], lnb_ref[...])
    y = y * jax.nn.sigmoid(y)
    o_ref[0] = (jnp.dot(y.astype(jnp.bfloat16), wp_ref[...],
                        preferred_element_type=jnp.float32) + bp_ref[...])


def _conv_call(x, w_in_bf16, b_glu, dw_kernel, dw_bias, ln_g, ln_b, w_pw2, b_pw2):
    B, S, _ = x.shape
    ts = ROW_TILE
    row = lambda b, i: (b, i, 0)
    const = lambda b, i: (0, 0)
    vec = pl.BlockSpec((1, D_MODEL), const)
    glu_col = GLU_COL0 // D_MODEL
    return pl.pallas_call(
        _conv_kernel,
        grid=(B, S // ts),
        in_specs=[pl.BlockSpec((1, ts, D_MODEL), row),
                  pl.BlockSpec((D_MODEL, D_MODEL), lambda b, i: (0, glu_col)),
                  pl.BlockSpec((D_MODEL, D_MODEL), lambda b, i: (0, glu_col + 1)),
                  pl.BlockSpec((1, 2 * D_MODEL), const),
                  pl.BlockSpec((CONV_WIDTH, D_MODEL), const),
                  vec, vec, vec,
                  pl.BlockSpec((D_MODEL, D_MODEL), const),
                  vec],
        out_specs=pl.BlockSpec((1, ts, D_MODEL), row),
        out_shape=jax.ShapeDtypeStruct((B, S, D_MODEL), jnp.float32),
        scratch_shapes=[pltpu.VMEM((ROW_TILE + CONV_HALO + CONV_PAD, D_MODEL), jnp.float32),
                        pltpu.VMEM((ROW_TILE, D_MODEL), jnp.float32)],
        compiler_params=pltpu.CompilerParams(
            dimension_semantics=("arbitrary", "arbitrary"), vmem_limit_bytes=VMEM_LIMIT),
        name="conv_branch",
    )(x, w_in_bf16, w_in_bf16, b_glu, dw_kernel, dw_bias, ln_g, ln_b, w_pw2, b_pw2)


def _merge_mlp_kernel(x_ref, att_ref, conv_ref, wga_ref, wgc_ref, bgate_ref, wout_ref,
                      l1g_ref, l1b_ref, w1_ref, w2_ref, l2g_ref, l2b_ref, o_ref):
    half_rows = ROW_TILE // 2

    def merge(rows):
        x = x_ref[rows, :]
        xb = x.astype(jnp.bfloat16)
        g_att = jax.nn.sigmoid(jnp.dot(xb, wga_ref[...], preferred_element_type=jnp.float32)
                               + bgate_ref[:, :D_MODEL])
        g_conv = jax.nn.sigmoid(jnp.dot(xb, wgc_ref[...], preferred_element_type=jnp.float32)
                                + bgate_ref[:, D_MODEL:])
        mix = g_att * att_ref[rows, :] + g_conv * conv_ref[rows, :]
        mixed = jnp.dot(mix.astype(jnp.bfloat16), wout_ref[...], preferred_element_type=jnp.float32)
        return _layer_norm(DEEPNORM_ALPHA * x + mixed, l1g_ref[...], l1b_ref[...])

    def mlp(h1):
        hb = h1.astype(jnp.bfloat16)
        ff = None
        for c in range(D_FF // D_MODEL):
            a = jnp.dot(hb, w1_ref[:, c * D_MODEL:(c + 1) * D_MODEL],
                        preferred_element_type=jnp.float32)
            a = jnp.maximum(a, 0.0)
            a = (a * a).astype(jnp.bfloat16)
            part = jnp.dot(a, w2_ref[c * D_MODEL:(c + 1) * D_MODEL, :],
                           preferred_element_type=jnp.float32)
            ff = part if ff is None else ff + part
        return _layer_norm(DEEPNORM_ALPHA * h1 + ff, l2g_ref[...], l2b_ref[...])

    halves = [slice(h * half_rows, (h + 1) * half_rows) for h in range(2)]
    h1s = [merge(rows) for rows in halves]
    for rows, h1 in zip(halves, h1s):
        o_ref[rows, :] = mlp(h1)


def _merge_mlp_call(x2, att2, conv2, w_in_bf16, b_gate, w_out, ln1_g, ln1_b, w_ff1, w_ff2, ln2_g, ln2_b):
    T = x2.shape[0]
    tm = ROW_TILE
    row = lambda i: (i, 0)
    const = lambda i: (0, 0)
    wspec = lambda shape, index_map=const: pl.BlockSpec(shape, index_map, pipeline_mode=pl.Buffered(1))
    tile = pl.BlockSpec((tm, D_MODEL), row)
    vec = wspec((1, D_MODEL))
    gate_col = GATE_COL0 // D_MODEL
    return pl.pallas_call(
        _merge_mlp_kernel,
        grid=(T // tm,),
        in_specs=[tile, tile, tile,
                  wspec((D_MODEL, D_MODEL), lambda i: (0, gate_col)),
                  wspec((D_MODEL, D_MODEL), lambda i: (0, gate_col + 1)),
                  wspec((1, 2 * D_MODEL)),
                  wspec((D_MODEL, D_MODEL)), vec, vec,
                  wspec((D_MODEL, D_FF)), wspec((D_FF, D_MODEL)), vec, vec],
        out_specs=tile,
        out_shape=jax.ShapeDtypeStruct((T, D_MODEL), jnp.float32),
        compiler_params=pltpu.CompilerParams(
            dimension_semantics=("arbitrary",), vmem_limit_bytes=VMEM_LIMIT),
        name="merge_mlp",
    )(x2, att2, conv2, w_in_bf16, w_in_bf16, b_gate, w_out, ln1_g, ln1_b, w_ff1, w_ff2, ln2_g, ln2_b)


def kernel(x, positions, w_in, b_glu, b_gate, lambda_q1, lambda_k1, lambda_q2, lambda_k2, subln_g, dw_kernel, dw_bias, conv_ln_g, conv_ln_b, w_pw2, b_pw2, w_out, ln1_g, ln1_b, w_ff1, w_ff2, ln2_g, ln2_b):
    B, S, D = x.shape
    T = B * S
    bf16 = jnp.bfloat16
    inv_freq = ROPE_THETA ** (-jnp.arange(HALF, dtype=jnp.float32) * 2.0 / HEAD_DIM)
    inv_col = inv_freq[:, None]
    pos3 = positions.astype(jnp.float32).reshape(B, 1, S)
    x2 = x.reshape(T, D)

    l = 0
    w = w_in[l].astype(bf16)
    wt_qkv = w[:, :GLU_COL0].T
    row = lambda a: a[l][None, :]

    qt, k, vt = _qkv_call(x, pos3, inv_col, wt_qkv)
    lam_params = jnp.stack([lambda_q1[l], lambda_k1[l], lambda_q2[l], lambda_k2[l]])
    att = _attn_call(lam_params, row(subln_g), qt, k, vt)
    conv = _conv_call(x, w, row(b_glu), dw_kernel[l], row(dw_bias),
                      row(conv_ln_g), row(conv_ln_b), w_pw2[l].astype(bf16), row(b_pw2))
    out = _merge_mlp_call(x2, att.reshape(T, D), conv.reshape(T, D), w, row(b_gate),
                          w_out[l].astype(bf16), row(ln1_g), row(ln1_b),
                          w_ff1[l].astype(bf16), w_ff2[l].astype(bf16), row(ln2_g), row(ln2_b))
    return out.reshape(B, S, D)
```

```python
import functools
import math

import jax
import jax.numpy as jnp
import numpy as np
from jax import lax
from jax.experimental import pallas as pl
from jax.experimental.pallas import tpu as pltpu

D_MODEL = 1024
N_HEADS = 8
HEAD_DIM = 64
HALF = HEAD_DIM // 2
V_DIM = 2 * HEAD_DIM
CHUNK = 64
CONV_WIDTH = 31
D_FF = 4 * D_MODEL
ROPE_THETA = 10000.0
LN_EPS = 1e-5
DEPTH = 1
DEEPNORM_ALPHA = (2.0 * DEPTH) ** 0.25
LAMBDA_INIT = 0.8 - 0.6 * math.exp(-0.3 * 0)
GLU_COL0 = 3 * D_MODEL
GATE_COL0 = 5 * D_MODEL

LANES = 128
VMEM_LIMIT = 56 * 1024 * 1024

ROW_TILE = 512
QKV_ROW_TILE = 1024
Q_TILE = 512
K_TILE = 512
KV_UNROLL = 6
ONES_ROWS = 16
PEN_ROWS = 16
CONV_HALO = 32
CONV_PAD = 8
CONV_ROWS = 64
NEG_BIG = -1e30
LOG2_E = math.log2(math.e)

_NT = (((1,), (1,)), ((), ()))


def _layer_norm(y, g, b):
    mu = jnp.mean(y, axis=-1, keepdims=True)
    d = y - mu
    var = jnp.mean(d * d, axis=-1, keepdims=True)
    return d * lax.rsqrt(var + LN_EPS) * g + b


def _rope_rows(t, cos, sin):
    pieces = []
    for g in range(D_MODEL // HEAD_DIM):
        t1 = t[g * HEAD_DIM:g * HEAD_DIM + HALF]
        t2 = t[g * HEAD_DIM + HALF:(g + 1) * HEAD_DIM]
        pieces.append(t1 * cos - t2 * sin)
        pieces.append(t1 * sin + t2 * cos)
    return jnp.concatenate(pieces, axis=0)


def _qkv_kernel(x_ref, pos_ref, inv_ref, wt_ref, qt_ref, k_ref, vt_ref):
    tm = x_ref.shape[1]
    xb = x_ref[0].astype(jnp.bfloat16)
    ang = inv_ref[...] * pos_ref[0]
    cos = jnp.cos(ang)
    sin = jnp.sin(ang)
    scale = HEAD_DIM ** -0.5 * LOG2_E

    qt = lax.dot_general(wt_ref[0:D_MODEL, :], xb, _NT,
                         preferred_element_type=jnp.float32)
    qt = _rope_rows(qt, cos * scale, sin * scale).astype(qt_ref.dtype)
    for j in range(tm // Q_TILE):
        qt_ref[0, j] = qt[:, j * Q_TILE:(j + 1) * Q_TILE]

    kt = lax.dot_general(wt_ref[D_MODEL:2 * D_MODEL, :], xb, _NT,
                         preferred_element_type=jnp.float32)
    k_ref[0] = _rope_rows(kt, cos, sin).T.astype(k_ref.dtype)

    vt = lax.dot_general(wt_ref[2 * D_MODEL:3 * D_MODEL, :], xb, _NT,
                         preferred_element_type=jnp.float32).astype(vt_ref.dtype)
    for j in range(tm // K_TILE):
        vt_ref[0, j] = vt[:, j * K_TILE:(j + 1) * K_TILE]


def _qkv_call(x, pos3, inv_col, wt_qkv):
    B, S, _ = x.shape
    tm = QKV_ROW_TILE
    const = lambda b, i: (0, 0)
    return pl.pallas_call(
        _qkv_kernel,
        grid=(B, S // tm),
        in_specs=[pl.BlockSpec((1, tm, D_MODEL), lambda b, i: (b, i, 0)),
                  pl.BlockSpec((1, 1, tm), lambda b, i: (b, 0, i)),
                  pl.BlockSpec((HALF, 1), const),
                  pl.BlockSpec((3 * D_MODEL, D_MODEL), const)],
        out_specs=[pl.BlockSpec((1, tm // Q_TILE, D_MODEL, Q_TILE), lambda b, i: (b, i, 0, 0)),
                   pl.BlockSpec((1, tm, D_MODEL), lambda b, i: (b, i, 0)),
                   pl.BlockSpec((1, tm // K_TILE, D_MODEL, K_TILE), lambda b, i: (b, i, 0, 0))],
        out_shape=[jax.ShapeDtypeStruct((B, S // Q_TILE, D_MODEL, Q_TILE), jnp.bfloat16),
                   jax.ShapeDtypeStruct((B, S, D_MODEL), jnp.bfloat16),
                   jax.ShapeDtypeStruct((B, S // K_TILE, D_MODEL, K_TILE), jnp.bfloat16)],
        compiler_params=pltpu.CompilerParams(
            dimension_semantics=("arbitrary", "arbitrary"), vmem_limit_bytes=VMEM_LIMIT),
        name="qkv_rope",
    )(x, pos3, inv_col, wt_qkv)


def _attn_schedule(nq):
    tiles_per_q = Q_TILE // K_TILE
    qs, ks = [], []
    for qi in range(nq):
        for kj in range((qi + 1) * tiles_per_q):
            qs.append(qi)
            ks.append(kj)
    return np.asarray(qs, np.int32), np.asarray(ks, np.int32)


def _attn_kernel(qtab_ref, ktab_ref, lam_ref, g_ref, qt_ref, k_ref, vt_ref, o_ref,
                 s_ref, m_ref, l_ref, acc_ref, *, n_steps, unroll):
    nq = qt_ref.shape[1]
    n_groups = n_steps // unroll
    chunks_per_k = K_TILE // CHUNK
    chunks_per_q = Q_TILE // CHUNK
    lp = lam_ref[...]
    lam = (jnp.exp(jnp.sum(lp[0:1] * lp[1:2], axis=-1, keepdims=True))
           - jnp.exp(jnp.sum(lp[2:3] * lp[3:4], axis=-1, keepdims=True))
           + LAMBDA_INIT)
    qry_chunk = lax.broadcasted_iota(jnp.int32, (1, Q_TILE), 1) // CHUNK
    pen_row = lax.broadcasted_iota(jnp.int32, (PEN_ROWS, Q_TILE), 0)
    key_row_chunk = lax.broadcasted_iota(jnp.int32, (K_TILE, LANES), 0) // CHUNK
    key_lane = lax.broadcasted_iota(jnp.int32, (K_TILE, LANES), 1)
    chunk_onehot = jnp.where(key_row_chunk == key_lane, 1.0, 0.0).astype(jnp.bfloat16)
    ones_rows = jnp.ones((ONES_ROWS, K_TILE), jnp.bfloat16)
    zeros = jnp.zeros((LANES, Q_TILE), jnp.bfloat16)

    m_ref[...] = jnp.full_like(m_ref, NEG_BIG)
    l_ref[...] = jnp.zeros_like(l_ref)
    acc_ref[...] = jnp.zeros_like(acc_ref)

    def scores(g, slot):
        for u in range(unroll):
            qi = qtab_ref[g * unroll + u]
            kj = ktab_ref[g * unroll + u]
            qt = qt_ref[0, qi]
            k0 = pl.multiple_of(kj * K_TILE, K_TILE)
            kt = jnp.concatenate([k_ref[0, pl.ds(k0, K_TILE), :], chunk_onehot], axis=1)
            hidden = (qry_chunk + qi * chunks_per_q) < (kj * chunks_per_k + pen_row)
            pen = jnp.where(hidden & (pen_row < chunks_per_k), NEG_BIG, 0.0).astype(jnp.bfloat16)
            pen = jnp.concatenate([pen, zeros[:LANES - PEN_ROWS]], axis=0)
            w0 = jnp.concatenate([qt[:HEAD_DIM], zeros[:HEAD_DIM], pen], axis=0)
            w1 = jnp.concatenate([zeros[:HEAD_DIM], qt[HEAD_DIM:], pen], axis=0)
            s_ref[slot, u, 0] = jnp.dot(kt, w0, preferred_element_type=jnp.float32)
            s_ref[slot, u, 1] = jnp.dot(kt, w1, preferred_element_type=jnp.float32)

    def softmax_pv(g, slot):
        for u in range(unroll):
            qi = qtab_ref[g * unroll + u]
            kj = ktab_ref[g * unroll + u]
            vt1 = jnp.concatenate([vt_ref[0, kj], ones_rows], axis=0)
            for c in range(2):
                s = s_ref[slot, u, c]
                m = m_ref[qi, c]
                m_new = jnp.maximum(m, jnp.max(s, axis=0, keepdims=True))
                p = jnp.exp2(s - m_new)
                alpha = jnp.exp2(m - m_new)
                pv = jnp.dot(vt1, p.astype(jnp.bfloat16), preferred_element_type=jnp.float32)
                m_ref[qi, c] = m_new
                l_ref[qi, c] = alpha * l_ref[qi, c] + pv[V_DIM:V_DIM + 1]
                acc_ref[qi, c] = alpha * acc_ref[qi, c] + pv[:V_DIM]

    scores(0, 0)

    def group_pair(j, carry):
        scores(2 * j + 1, 1)
        softmax_pv(2 * j, 0)
        scores(jnp.minimum(2 * j + 2, n_groups - 1), 0)
        softmax_pv(2 * j + 1, 1)
        return carry

    lax.fori_loop(0, n_groups // 2, group_pair, 0)

    def finish(qi, carry):
        o = acc_ref[qi, 0] * (1.0 / l_ref[qi, 0]) - acc_ref[qi, 1] * (lam / l_ref[qi, 1])
        y = o * lax.rsqrt(jnp.mean(o * o, axis=0, keepdims=True) + LN_EPS)
        q0 = pl.multiple_of(qi * Q_TILE, Q_TILE)
        o_ref[0, pl.ds(q0, Q_TILE), :] = y.T * (g_ref[...] * (1.0 - LAMBDA_INIT))
        return carry

    lax.fori_loop(0, nq, finish, 0, unroll=4)


def _attn_call(lam_params, subln_g, qt, k, vt):
    B, S, _ = k.shape
    nq = S // Q_TILE
    qtab, ktab = _attn_schedule(nq)
    n_steps = len(qtab)
    unroll = max(u for u in range(1, KV_UNROLL + 1) if n_steps % (2 * u) == 0)
    const = lambda b, h, *_: (0, 0)
    grid_spec = pltpu.PrefetchScalarGridSpec(
        num_scalar_prefetch=2,
        grid=(B, N_HEADS),
        in_specs=[pl.BlockSpec((4, HEAD_DIM), const),
                  pl.BlockSpec((1, V_DIM), const),
                  pl.BlockSpec((1, nq, V_DIM, Q_TILE), lambda b, h, *_: (b, 0, h, 0)),
                  pl.BlockSpec((1, S, LANES), lambda b, h, *_: (b, 0, h)),
                  pl.BlockSpec((1, S // K_TILE, V_DIM, K_TILE), lambda b, h, *_: (b, 0, h, 0))],
        out_specs=pl.BlockSpec((1, S, LANES), lambda b, h, *_: (b, 0, h)),
        scratch_shapes=[pltpu.VMEM((2, unroll, 2, K_TILE, Q_TILE), jnp.float32),
                        pltpu.VMEM((nq, 2, 1, Q_TILE), jnp.float32),
                        pltpu.VMEM((nq, 2, 1, Q_TILE), jnp.float32),
                        pltpu.VMEM((nq, 2, V_DIM, Q_TILE), jnp.float32)])
    return pl.pallas_call(
        functools.partial(_attn_kernel, n_steps=n_steps, unroll=unroll),
        grid_spec=grid_spec,
        out_shape=jax.ShapeDtypeStruct((B, S, D_MODEL), jnp.float32),
        compiler_params=pltpu.CompilerParams(
            dimension_semantics=("arbitrary", "arbitrary"), vmem_limit_bytes=VMEM_LIMIT),
        name="diff_attn",
    )(jnp.asarray(qtab), jnp.asarray(ktab), lam_params, subln_g, qt, k, vt)


def _depthwise_conv(ext_ref, dwk_ref, dwb_ref, out_ref, ts):
    base = CONV_HALO - (CONV_WIDTH - 1)
    groups = [[(j, (base + j) // 8) for j in range(CONV_WIDTH) if (base + j) % 8 == r]
              for r in range(8)]
    max_a = max(a for g in groups for _, a in g)
    rows = CONV_ROWS + CONV_PAD
    assert rows + 8 * max_a <= CONV_ROWS + CONV_HALO + CONV_PAD

    def col_body(c, carry):
        lanes = pl.ds(pl.multiple_of(c * LANES, LANES), LANES)
        kcol = dwk_ref[:, lanes]
        bias = dwb_ref[:, lanes]
        for rb in range(ts // CONV_ROWS):
            t0 = rb * CONV_ROWS
            e = ext_ref[t0:t0 + rows + 8 * max_a, lanes]
            w = None
            for r in range(7, -1, -1):
                v = None
                for j, a in groups[r]:
                    term = e[8 * a:8 * a + rows] * kcol[j:j + 1]
                    v = term if v is None else v + term
                if w is not None:
                    v = v + pltpu.roll(w, rows - 1, 0)
                w = v
            out_ref[t0:t0 + CONV_ROWS, lanes] = w[:CONV_ROWS] + bias
        return carry

    lax.fori_loop(0, D_MODEL // LANES, col_body, 0)


def _conv_kernel(x_ref, wga_ref, wgb_ref, bg_ref, dwk_ref, dwb_ref, lng_ref, lnb_ref,
                 wp_ref, bp_ref, o_ref, ext_ref, acc_ref):
    ts = x_ref.shape[1]

    @pl.when(pl.program_id(1) == 0)
    def _():
        ext_ref[0:CONV_HALO, :] = jnp.zeros((CONV_HALO, D_MODEL), jnp.float32)

    xb = x_ref[0].astype(jnp.bfloat16)
    ga = jnp.dot(xb, wga_ref[...], preferred_element_type=jnp.float32) + bg_ref[:, :D_MODEL]
    gb = jnp.dot(xb, wgb_ref[...], preferred_element_type=jnp.float32) + bg_ref[:, D_MODEL:]
    ext_ref[CONV_HALO:CONV_HALO + ts, :] = ga * jax.nn.sigmoid(gb)
    ext_ref[CONV_HALO + ts:, :] = jnp.zeros((CONV_PAD, D_MODEL), jnp.float32)
    _depthwise_conv(ext_ref, dwk_ref, dwb_ref, acc_ref, ts)
    ext_ref[0:CONV_HALO, :] = ext_ref[ts:ts + CONV_HALO, :]
    y = _layer_norm(acc_ref[...], lng_ref[...], lnb_ref[...])
    y = y * jax.nn.sigmoid(y)
    o_ref[0] = (jnp.dot(y.astype(jnp.bfloat16), wp_ref[...],
                        preferred_element_type=jnp.float32) + bp_ref[...])


def _conv_call(x, w_in_bf16, b_glu, dw_kernel, dw_bias, ln_g, ln_b, w_pw2, b_pw2):
    B, S, _ = x.shape
    ts = ROW_TILE
    row = lambda b, i: (b, i, 0)
    const = lambda b, i: (0, 0)
    vec = pl.BlockSpec((1, D_MODEL), const)
    glu_col = GLU_COL0 // D_MODEL
    return pl.pallas_call(
        _conv_kernel,
        grid=(B, S // ts),
        in_specs=[pl.BlockSpec((1, ts, D_MODEL), row),
                  pl.BlockSpec((D_MODEL, D_MODEL), lambda b, i: (0, glu_col)),
                  pl.BlockSpec((D_MODEL, D_MODEL), lambda b, i: (0, glu_col + 1)),
                  pl.BlockSpec((1, 2 * D_MODEL), const),
                  pl.BlockSpec((CONV_WIDTH, D_MODEL), const),
                  vec, vec, vec,
                  pl.BlockSpec((D_MODEL, D_MODEL), const),
                  vec],
        out_specs=pl.BlockSpec((1, ts, D_MODEL), row),
        out_shape=jax.ShapeDtypeStruct((B, S, D_MODEL), jnp.float32),
        scratch_shapes=[pltpu.VMEM((ROW_TILE + CONV_HALO + CONV_PAD, D_MODEL), jnp.float32),
                        pltpu.VMEM((ROW_TILE, D_MODEL), jnp.float32)],
        compiler_params=pltpu.CompilerParams(
            dimension_semantics=("arbitrary", "arbitrary"), vmem_limit_bytes=VMEM_LIMIT),
        name="conv_branch",
    )(x, w_in_bf16, w_in_bf16, b_glu, dw_kernel, dw_bias, ln_g, ln_b, w_pw2, b_pw2)


def _merge_mlp_kernel(x_ref, att_ref, conv_ref, wga_ref, wgc_ref, bgate_ref, wout_ref,
                      l1g_ref, l1b_ref, w1_ref, w2_ref, l2g_ref, l2b_ref, o_ref):
    half_rows = ROW_TILE // 2

    def merge(rows):
        x = x_ref[rows, :]
        xb = x.astype(jnp.bfloat16)
        g_att = jax.nn.sigmoid(jnp.dot(xb, wga_ref[...], preferred_element_type=jnp.float32)
                               + bgate_ref[:, :D_MODEL])
        g_conv = jax.nn.sigmoid(jnp.dot(xb, wgc_ref[...], preferred_element_type=jnp.float32)
                                + bgate_ref[:, D_MODEL:])
        mix = g_att * att_ref[rows, :] + g_conv * conv_ref[rows, :]
        mixed = jnp.dot(mix.astype(jnp.bfloat16), wout_ref[...], preferred_element_type=jnp.float32)
        return _layer_norm(DEEPNORM_ALPHA * x + mixed, l1g_ref[...], l1b_ref[...])

    def mlp(h1):
        hb = h1.astype(jnp.bfloat16)
        ff = None
        for c in range(D_FF // D_MODEL):
            a = jnp.dot(hb, w1_ref[:, c * D_MODEL:(c + 1) * D_MODEL],
                        preferred_element_type=jnp.float32)
            a = jnp.maximum(a, 0.0)
            a = (a * a).astype(jnp.bfloat16)
            part = jnp.dot(a, w2_ref[c * D_MODEL:(c + 1) * D_MODEL, :],
                           preferred_element_type=jnp.float32)
            ff = part if ff is None else ff + part
        return _layer_norm(DEEPNORM_ALPHA * h1 + ff, l2g_ref[...], l2b_ref[...])

    halves = [slice(h * half_rows, (h + 1) * half_rows) for h in range(2)]
    h1s = [merge(rows) for rows in halves]
    for rows, h1 in zip(halves, h1s):
        o_ref[rows, :] = mlp(h1)


def _merge_mlp_call(x2, att2, conv2, w_in_bf16, b_gate, w_out, ln1_g, ln1_b, w_ff1, w_ff2, ln2_g, ln2_b):
    T = x2.shape[0]
    tm = ROW_TILE
    row = lambda i: (i, 0)
    const = lambda i: (0, 0)
    wspec = lambda shape, index_map=const: pl.BlockSpec(shape, index_map, pipeline_mode=pl.Buffered(1))
    tile = pl.BlockSpec((tm, D_MODEL), row)
    vec = wspec((1, D_MODEL))
    gate_col = GATE_COL0 // D_MODEL
    return pl.pallas_call(
        _merge_mlp_kernel,
        grid=(T // tm,),
        in_specs=[tile, tile, tile,
                  wspec((D_MODEL, D_MODEL), lambda i: (0, gate_col)),
                  wspec((D_MODEL, D_MODEL), lambda i: (0, gate_col + 1)),
                  wspec((1, 2 * D_MODEL)),
                  wspec((D_MODEL, D_MODEL)), vec, vec,
                  wspec((D_MODEL, D_FF)), wspec((D_FF, D_MODEL)), vec, vec],
        out_specs=tile,
        out_shape=jax.ShapeDtypeStruct((T, D_MODEL), jnp.float32),
        compiler_params=pltpu.CompilerParams(
            dimension_semantics=("arbitrary",), vmem_limit_bytes=VMEM_LIMIT),
        name="merge_mlp",
    )(x2, att2, conv2, w_in_bf16, w_in_bf16, b_gate, w_out, ln1_g, ln1_b, w_ff1, w_ff2, ln2_g, ln2_b)


def kernel(x, positions, w_in, b_glu, b_gate, lambda_q1, lambda_k1, lambda_q2, lambda_k2, subln_g, dw_kernel, dw_bias, conv_ln_g, conv_ln_b, w_pw2, b_pw2, w_out, ln1_g, ln1_b, w_ff1, w_ff2, ln2_g, ln2_b):
    B, S, D = x.shape
    T = B * S
    bf16 = jnp.bfloat16
    inv_freq = ROPE_THETA ** (-jnp.arange(HALF, dtype=jnp.float32) * 2.0 / HEAD_DIM)
    inv_col = inv_freq[:, None]
    pos3 = positions.astype(jnp.float32).reshape(B, 1, S)
    x2 = x.reshape(T, D)

    l = 0
    w = w_in[l].astype(bf16)
    wt_qkv = w[:, :GLU_COL0].T
    row = lambda a: a[l][None, :]

    qt, k, vt = _qkv_call(x, pos3, inv_col, wt_qkv)
    lam_params = jnp.stack([lambda_q1[l], lambda_k1[l], lambda_q2[l], lambda_k2[l]])
    att = _attn_call(lam_params, row(subln_g), qt, k, vt)
    conv = _conv_call(x, w, row(b_glu), dw_kernel[l], row(dw_bias),
                      row(conv_ln_g), row(conv_ln_b), w_pw2[l].astype(bf16), row(b_pw2))
    out = _merge_mlp_call(x2, att.reshape(T, D), conv.reshape(T, D), w, row(b_gate),
                          w_out[l].astype(bf16), row(ln1_g), row(ln1_b),
                          w_ff1[l].astype(bf16), w_ff2[l].astype(bf16), row(ln2_g), row(ln2_b))
    return out.reshape(B, S, D)
```

```python
import functools
import math

import jax
import jax.numpy as jnp
import numpy as np
from jax import lax
from jax.experimental import pallas as pl
from jax.experimental.pallas import tpu as pltpu

D_MODEL = 1024
N_HEADS = 8
HEAD_DIM = 64
HALF = HEAD_DIM // 2
V_DIM = 2 * HEAD_DIM
CHUNK = 64
CONV_WIDTH = 31
D_FF = 4 * D_MODEL
ROPE_THETA = 10000.0
LN_EPS = 1e-5
DEPTH = 1
DEEPNORM_ALPHA = (2.0 * DEPTH) ** 0.25
LAMBDA_INIT = 0.8 - 0.6 * math.exp(-0.3 * 0)
GLU_COL0 = 3 * D_MODEL
GATE_COL0 = 5 * D_MODEL

LANES = 128
VMEM_LIMIT = 56 * 1024 * 1024

ROW_TILE = 512
QKV_ROW_TILE = 1024
Q_TILE = 512
K_TILE = 512
KV_UNROLL = 2
ONES_ROWS = 16
PEN_ROWS = 16
CONV_HALO = 32
CONV_PAD = 8
CONV_ROWS = 64
NEG_BIG = -1e30
LOG2_E = math.log2(math.e)

_NT = (((1,), (1,)), ((), ()))


def _layer_norm(y, g, b):
    mu = jnp.mean(y, axis=-1, keepdims=True)
    d = y - mu
    var = jnp.mean(d * d, axis=-1, keepdims=True)
    return d * lax.rsqrt(var + LN_EPS) * g + b


def _rope_rows(t, cos, sin):
    pieces = []
    for g in range(D_MODEL // HEAD_DIM):
        t1 = t[g * HEAD_DIM:g * HEAD_DIM + HALF]
        t2 = t[g * HEAD_DIM + HALF:(g + 1) * HEAD_DIM]
        pieces.append(t1 * cos - t2 * sin)
        pieces.append(t1 * sin + t2 * cos)
    return jnp.concatenate(pieces, axis=0)


def _qkv_kernel(x_ref, pos_ref, inv_ref, wt_ref, qt_ref, k_ref, vt_ref):
    tm = x_ref.shape[1]
    xb = x_ref[0].astype(jnp.bfloat16)
    ang = inv_ref[...] * pos_ref[0]
    cos = jnp.cos(ang)
    sin = jnp.sin(ang)
    scale = HEAD_DIM ** -0.5 * LOG2_E

    qt = lax.dot_general(wt_ref[0:D_MODEL, :], xb, _NT,
                         preferred_element_type=jnp.float32)
    qt = _rope_rows(qt, cos * scale, sin * scale).astype(qt_ref.dtype)
    for j in range(tm // Q_TILE):
        qt_ref[0, j] = qt[:, j * Q_TILE:(j + 1) * Q_TILE]

    kt = lax.dot_general(wt_ref[D_MODEL:2 * D_MODEL, :], xb, _NT,
                         preferred_element_type=jnp.float32)
    k_ref[0] = _rope_rows(kt, cos, sin).T.astype(k_ref.dtype)

    vt = lax.dot_general(wt_ref[2 * D_MODEL:3 * D_MODEL, :], xb, _NT,
                         preferred_element_type=jnp.float32).astype(vt_ref.dtype)
    for j in range(tm // K_TILE):
        vt_ref[0, j] = vt[:, j * K_TILE:(j + 1) * K_TILE]


def _qkv_call(x, pos3, inv_col, wt_qkv):
    B, S, _ = x.shape
    tm = QKV_ROW_TILE
    const = lambda b, i: (0, 0)
    return pl.pallas_call(
        _qkv_kernel,
        grid=(B, S // tm),
        in_specs=[pl.BlockSpec((1, tm, D_MODEL), lambda b, i: (b, i, 0)),
                  pl.BlockSpec((1, 1, tm), lambda b, i: (b, 0, i)),
                  pl.BlockSpec((HALF, 1), const),
                  pl.BlockSpec((3 * D_MODEL, D_MODEL), const)],
        out_specs=[pl.BlockSpec((1, tm // Q_TILE, D_MODEL, Q_TILE), lambda b, i: (b, i, 0, 0)),
                   pl.BlockSpec((1, tm, D_MODEL), lambda b, i: (b, i, 0)),
                   pl.BlockSpec((1, tm // K_TILE, D_MODEL, K_TILE), lambda b, i: (b, i, 0, 0))],
        out_shape=[jax.ShapeDtypeStruct((B, S // Q_TILE, D_MODEL, Q_TILE), jnp.bfloat16),
                   jax.ShapeDtypeStruct((B, S, D_MODEL), jnp.bfloat16),
                   jax.ShapeDtypeStruct((B, S // K_TILE, D_MODEL, K_TILE), jnp.bfloat16)],
        compiler_params=pltpu.CompilerParams(
            dimension_semantics=("arbitrary", "arbitrary"), vmem_limit_bytes=VMEM_LIMIT),
        name="qkv_rope",
    )(x, pos3, inv_col, wt_qkv)


def _attn_schedule(nq):
    tiles_per_q = Q_TILE // K_TILE
    qs, ks = [], []
    for qi in range(nq):
        for kj in range((qi + 1) * tiles_per_q):
            qs.append(qi)
            ks.append(kj)
    return np.asarray(qs, np.int32), np.asarray(ks, np.int32)


def _attn_kernel(qtab_ref, ktab_ref, lam_ref, g_ref, qt_ref, k_ref, vt_ref, o_ref,
                 s_ref, m_ref, l_ref, acc_ref, *, n_steps, unroll):
    nq = qt_ref.shape[1]
    n_groups = n_steps // unroll
    chunks_per_k = K_TILE // CHUNK
    chunks_per_q = Q_TILE // CHUNK
    lp = lam_ref[...]
    lam = (jnp.exp(jnp.sum(lp[0:1] * lp[1:2], axis=-1, keepdims=True))
           - jnp.exp(jnp.sum(lp[2:3] * lp[3:4], axis=-1, keepdims=True))
           + LAMBDA_INIT)
    qry_chunk = lax.broadcasted_iota(jnp.int32, (1, Q_TILE), 1) // CHUNK
    pen_row = lax.broadcasted_iota(jnp.int32, (PEN_ROWS, Q_TILE), 0)
    key_row_chunk = lax.broadcasted_iota(jnp.int32, (K_TILE, LANES), 0) // CHUNK
    key_lane = lax.broadcasted_iota(jnp.int32, (K_TILE, LANES), 1)
    chunk_onehot = jnp.where(key_row_chunk == key_lane, 1.0, 0.0).astype(jnp.bfloat16)
    ones_rows = jnp.ones((ONES_ROWS, K_TILE), jnp.bfloat16)
    zeros = jnp.zeros((LANES, Q_TILE), jnp.bfloat16)

    m_ref[...] = jnp.full_like(m_ref, NEG_BIG)
    l_ref[...] = jnp.zeros_like(l_ref)
    acc_ref[...] = jnp.zeros_like(acc_ref)

    def scores(g, slot):
        for u in range(unroll):
            qi = qtab_ref[g * unroll + u]
            kj = ktab_ref[g * unroll + u]
            qt = qt_ref[0, qi]
            k0 = pl.multiple_of(kj * K_TILE, K_TILE)
            kt = jnp.concatenate([k_ref[0, pl.ds(k0, K_TILE), :], chunk_onehot], axis=1)
            hidden = (qry_chunk + qi * chunks_per_q) < (kj * chunks_per_k + pen_row)
            pen = jnp.where(hidden & (pen_row < chunks_per_k), NEG_BIG, 0.0).astype(jnp.bfloat16)
            pen = jnp.concatenate([pen, zeros[:LANES - PEN_ROWS]], axis=0)
            w0 = jnp.concatenate([qt[:HEAD_DIM], zeros[:HEAD_DIM], pen], axis=0)
            w1 = jnp.concatenate([zeros[:HEAD_DIM], qt[HEAD_DIM:], pen], axis=0)
            s_ref[slot, u, 0] = jnp.dot(kt, w0, preferred_element_type=jnp.float32)
            s_ref[slot, u, 1] = jnp.dot(kt, w1, preferred_element_type=jnp.float32)

    def softmax_pv(g, slot):
        for u in range(unroll):
            qi = qtab_ref[g * unroll + u]
            kj = ktab_ref[g * unroll + u]
            vt1 = jnp.concatenate([vt_ref[0, kj], ones_rows], axis=0)
            for c in range(2):
                s = s_ref[slot, u, c]
                m = m_ref[qi, c]
                m_new = jnp.maximum(m, jnp.max(s, axis=0, keepdims=True))
                p = jnp.exp2(s - m_new)
                alpha = jnp.exp2(m - m_new)
                pv = jnp.dot(vt1, p.astype(jnp.bfloat16), preferred_element_type=jnp.float32)
                m_ref[qi, c] = m_new
                l_ref[qi, c] = alpha * l_ref[qi, c] + pv[V_DIM:V_DIM + 1]
                acc_ref[qi, c] = alpha * acc_ref[qi, c] + pv[:V_DIM]

    scores(0, 0)

    def group_pair(j, carry):
        scores(2 * j + 1, 1)
        softmax_pv(2 * j, 0)
        scores(jnp.minimum(2 * j + 2, n_groups - 1), 0)
        softmax_pv(2 * j + 1, 1)
        return carry

    lax.fori_loop(0, n_groups // 2, group_pair, 0)

    def finish(qi, carry):
        o = acc_ref[qi, 0] * (1.0 / l_ref[qi, 0]) - acc_ref[qi, 1] * (lam / l_ref[qi, 1])
        y = o * lax.rsqrt(jnp.mean(o * o, axis=0, keepdims=True) + LN_EPS)
        q0 = pl.multiple_of(qi * Q_TILE, Q_TILE)
        o_ref[0, pl.ds(q0, Q_TILE), :] = y.T * (g_ref[...] * (1.0 - LAMBDA_INIT))
        return carry

    lax.fori_loop(0, nq, finish, 0, unroll=4)


def _attn_call(lam_params, subln_g, qt, k, vt):
    B, S, _ = k.shape
    nq = S // Q_TILE
    qtab, ktab = _attn_schedule(nq)
    n_steps = len(qtab)
    unroll = max(u for u in range(1, KV_UNROLL + 1) if n_steps % (2 * u) == 0)
    const = lambda b, h, *_: (0, 0)
    grid_spec = pltpu.PrefetchScalarGridSpec(
        num_scalar_prefetch=2,
        grid=(B, N_HEADS),
        in_specs=[pl.BlockSpec((4, HEAD_DIM), const),
                  pl.BlockSpec((1, V_DIM), const),
                  pl.BlockSpec((1, nq, V_DIM, Q_TILE), lambda b, h, *_: (b, 0, h, 0)),
                  pl.BlockSpec((1, S, LANES), lambda b, h, *_: (b, 0, h)),
                  pl.BlockSpec((1, S // K_TILE, V_DIM, K_TILE), lambda b, h, *_: (b, 0, h, 0))],
        out_specs=pl.BlockSpec((1, S, LANES), lambda b, h, *_: (b, 0, h)),
        scratch_shapes=[pltpu.VMEM((2, unroll, 2, K_TILE, Q_TILE), jnp.float32),
                        pltpu.VMEM((nq, 2, 1, Q_TILE), jnp.float32),
                        pltpu.VMEM((nq, 2, 1, Q_TILE), jnp.float32),
                        pltpu.VMEM((nq, 2, V_DIM, Q_TILE), jnp.float32)])
    return pl.pallas_call(
        functools.partial(_attn_kernel, n_steps=n_steps, unroll=unroll),
        grid_spec=grid_spec,
        out_shape=jax.ShapeDtypeStruct((B, S, D_MODEL), jnp.float32),
        compiler_params=pltpu.CompilerParams(
            dimension_semantics=("arbitrary", "arbitrary"), vmem_limit_bytes=VMEM_LIMIT),
        name="diff_attn",
    )(jnp.asarray(qtab), jnp.asarray(ktab), lam_params, subln_g, qt, k, vt)


def _depthwise_conv(ext_ref, dwk_ref, dwb_ref, out_ref, ts):
    base = CONV_HALO - (CONV_WIDTH - 1)
    groups = [[(j, (base + j) // 8) for j in range(CONV_WIDTH) if (base + j) % 8 == r]
              for r in range(8)]
    max_a = max(a for g in groups for _, a in g)
    rows = CONV_ROWS + CONV_PAD
    assert rows + 8 * max_a <= CONV_ROWS + CONV_HALO + CONV_PAD

    def col_body(c, carry):
        lanes = pl.ds(pl.multiple_of(c * LANES, LANES), LANES)
        kcol = dwk_ref[:, lanes]
        bias = dwb_ref[:, lanes]
        for rb in range(ts // CONV_ROWS):
            t0 = rb * CONV_ROWS
            e = ext_ref[t0:t0 + rows + 8 * max_a, lanes]
            w = None
            for r in range(7, -1, -1):
                v = None
                for j, a in groups[r]:
                    term = e[8 * a:8 * a + rows] * kcol[j:j + 1]
                    v = term if v is None else v + term
                if w is not None:
                    v = v + pltpu.roll(w, rows - 1, 0)
                w = v
            out_ref[t0:t0 + CONV_ROWS, lanes] = w[:CONV_ROWS] + bias
        return carry

    lax.fori_loop(0, D_MODEL // LANES, col_body, 0)


def _conv_kernel(x_ref, wga_ref, wgb_ref, bg_ref, dwk_ref, dwb_ref, lng_ref, lnb_ref,
                 wp_ref, bp_ref, o_ref, ext_ref, acc_ref):
    ts = x_ref.shape[1]

    @pl.when(pl.program_id(1) == 0)
    def _():
        ext_ref[0:CONV_HALO, :] = jnp.zeros((CONV_HALO, D_MODEL), jnp.float32)

    xb = x_ref[0].astype(jnp.bfloat16)
    ga = jnp.dot(xb, wga_ref[...], preferred_element_type=jnp.float32) + bg_ref[:, :D_MODEL]
    gb = jnp.dot(xb, wgb_ref[...], preferred_element_type=jnp.float32) + bg_ref[:, D_MODEL:]
    ext_ref[CONV_HALO:CONV_HALO + ts, :] = ga * jax.nn.sigmoid(gb)
    ext_ref[CONV_HALO + ts:, :] = jnp.zeros((CONV_PAD, D_MODEL), jnp.float32)
    _depthwise_conv(ext_ref, dwk_ref, dwb_ref, acc_ref, ts)
    ext_ref[0:CONV_HALO, :] = ext_ref[ts:ts + CONV_HALO, :]
    y = _layer_norm(acc_ref[...], lng_ref[...], lnb_ref[...])
    y = y * jax.nn.sigmoid(y)
    o_ref[0] = (jnp.dot(y.astype(jnp.bfloat16), wp_ref[...],
                        preferred_element_type=jnp.float32) + bp_ref[...])


def _conv_call(x, w_in_bf16, b_glu, dw_kernel, dw_bias, ln_g, ln_b, w_pw2, b_pw2):
    B, S, _ = x.shape
    ts = ROW_TILE
    row = lambda b, i: (b, i, 0)
    const = lambda b, i: (0, 0)
    vec = pl.BlockSpec((1, D_MODEL), const)
    glu_col = GLU_COL0 // D_MODEL
    return pl.pallas_call(
        _conv_kernel,
        grid=(B, S // ts),
        in_specs=[pl.BlockSpec((1, ts, D_MODEL), row),
                  pl.BlockSpec((D_MODEL, D_MODEL), lambda b, i: (0, glu_col)),
                  pl.BlockSpec((D_MODEL, D_MODEL), lambda b, i: (0, glu_col + 1)),
                  pl.BlockSpec((1, 2 * D_MODEL), const),
                  pl.BlockSpec((CONV_WIDTH, D_MODEL), const),
                  vec, vec, vec,
                  pl.BlockSpec((D_MODEL, D_MODEL), const),
                  vec],
        out_specs=pl.BlockSpec((1, ts, D_MODEL), row),
        out_shape=jax.ShapeDtypeStruct((B, S, D_MODEL), jnp.float32),
        scratch_shapes=[pltpu.VMEM((ROW_TILE + CONV_HALO + CONV_PAD, D_MODEL), jnp.float32),
                        pltpu.VMEM((ROW_TILE, D_MODEL), jnp.float32)],
        compiler_params=pltpu.CompilerParams(
            dimension_semantics=("arbitrary", "arbitrary"), vmem_limit_bytes=VMEM_LIMIT),
        name="conv_branch",
    )(x, w_in_bf16, w_in_bf16, b_glu, dw_kernel, dw_bias, ln_g, ln_b, w_pw2, b_pw2)


def _merge_mlp_kernel(x_ref, att_ref, conv_ref, wga_ref, wgc_ref, bgate_ref, wout_ref,
                      l1g_ref, l1b_ref, w1_ref, w2_ref, l2g_ref, l2b_ref, o_ref):
    half_rows = ROW_TILE // 2

    def merge(rows):
        x = x_ref[rows, :]
        xb = x.astype(jnp.bfloat16)
        g_att = jax.nn.sigmoid(jnp.dot(xb, wga_ref[...], preferred_element_type=jnp.float32)
                               + bgate_ref[:, :D_MODEL])
        g_conv = jax.nn.sigmoid(jnp.dot(xb, wgc_ref[...], preferred_element_type=jnp.float32)
                                + bgate_ref[:, D_MODEL:])
        mix = g_att * att_ref[rows, :] + g_conv * conv_ref[rows, :]
        mixed = jnp.dot(mix.astype(jnp.bfloat16), wout_ref[...], preferred_element_type=jnp.float32)
        return _layer_norm(DEEPNORM_ALPHA * x + mixed, l1g_ref[...], l1b_ref[...])

    def mlp(h1):
        hb = h1.astype(jnp.bfloat16)
        ff = None
        for c in range(D_FF // D_MODEL):
            a = jnp.dot(hb, w1_ref[:, c * D_MODEL:(c + 1) * D_MODEL],
                        preferred_element_type=jnp.float32)
            a = jnp.maximum(a, 0.0)
            a = (a * a).astype(jnp.bfloat16)
            part = jnp.dot(a, w2_ref[c * D_MODEL:(c + 1) * D_MODEL, :],
                           preferred_element_type=jnp.float32)
            ff = part if ff is None else ff + part
        return _layer_norm(DEEPNORM_ALPHA * h1 + ff, l2g_ref[...], l2b_ref[...])

    halves = [slice(h * half_rows, (h + 1) * half_rows) for h in range(2)]
    h1s = [merge(rows) for rows in halves]
    for rows, h1 in zip(halves, h1s):
        o_ref[rows, :] = mlp(h1)


def _merge_mlp_call(x2, att2, conv2, w_in_bf16, b_gate, w_out, ln1_g, ln1_b, w_ff1, w_ff2, ln2_g, ln2_b):
    T = x2.shape[0]
    tm = ROW_TILE
    row = lambda i: (i, 0)
    const = lambda i: (0, 0)
    wspec = lambda shape, index_map=const: pl.BlockSpec(shape, index_map, pipeline_mode=pl.Buffered(1))
    tile = pl.BlockSpec((tm, D_MODEL), row)
    vec = wspec((1, D_MODEL))
    gate_col = GATE_COL0 // D_MODEL
    return pl.pallas_call(
        _merge_mlp_kernel,
        grid=(T // tm,),
        in_specs=[tile, tile, tile,
                  wspec((D_MODEL, D_MODEL), lambda i: (0, gate_col)),
                  wspec((D_MODEL, D_MODEL), lambda i: (0, gate_col + 1)),
                  wspec((1, 2 * D_MODEL)),
                  wspec((D_MODEL, D_MODEL)), vec, vec,
                  wspec((D_MODEL, D_FF)), wspec((D_FF, D_MODEL)), vec, vec],
        out_specs=tile,
        out_shape=jax.ShapeDtypeStruct((T, D_MODEL), jnp.float32),
        compiler_params=pltpu.CompilerParams(
            dimension_semantics=("arbitrary",), vmem_limit_bytes=VMEM_LIMIT),
        name="merge_mlp",
    )(x2, att2, conv2, w_in_bf16, w_in_bf16, b_gate, w_out, ln1_g, ln1_b, w_ff1, w_ff2, ln2_g, ln2_b)


def kernel(x, positions, w_in, b_glu, b_gate, lambda_q1, lambda_k1, lambda_q2, lambda_k2, subln_g, dw_kernel, dw_bias, conv_ln_g, conv_ln_b, w_pw2, b_pw2, w_out, ln1_g, ln1_b, w_ff1, w_ff2, ln2_g, ln2_b):
    B, S, D = x.shape
    T = B * S
    bf16 = jnp.bfloat16
    inv_freq = ROPE_THETA ** (-jnp.arange(HALF, dtype=jnp.float32) * 2.0 / HEAD_DIM)
    inv_col = inv_freq[:, None]
    pos3 = positions.astype(jnp.float32).reshape(B, 1, S)
    x2 = x.reshape(T, D)

    l = 0
    w = w_in[l].astype(bf16)
    wt_qkv = w[:, :GLU_COL0].T
    row = lambda a: a[l][None, :]

    qt, k, vt = _qkv_call(x, pos3, inv_col, wt_qkv)
    lam_params = jnp.stack([lambda_q1[l], lambda_k1[l], lambda_q2[l], lambda_k2[l]])
    att = _attn_call(lam_params, row(subln_g), qt, k, vt)
    conv = _conv_call(x, w, row(b_glu), dw_kernel[l], row(dw_bias),
                      row(conv_ln_g), row(conv_ln_b), w_pw2[l].astype(bf16), row(b_pw2))
    out = _merge_mlp_call(x2, att.reshape(T, D), conv.reshape(T, D), w, row(b_gate),
                          w_out[l].astype(bf16), row(ln1_g), row(ln1_b),
                          w_ff1[l].astype(bf16), w_ff2[l].astype(bf16), row(ln2_g), row(ln2_b))
    return out.reshape(B, S, D)
```

```python
import functools
import math

import jax
import jax.numpy as jnp
import numpy as np
from jax import lax
from jax.experimental import pallas as pl
from jax.experimental.pallas import tpu as pltpu

D_MODEL = 1024
N_HEADS = 8
HEAD_DIM = 64
HALF = HEAD_DIM // 2
V_DIM = 2 * HEAD_DIM
CHUNK = 64
CONV_WIDTH = 31
D_FF = 4 * D_MODEL
ROPE_THETA = 10000.0
LN_EPS = 1e-5
DEPTH = 1
DEEPNORM_ALPHA = (2.0 * DEPTH) ** 0.25
LAMBDA_INIT = 0.8 - 0.6 * math.exp(-0.3 * 0)
GLU_COL0 = 3 * D_MODEL
GATE_COL0 = 5 * D_MODEL

LANES = 128
VMEM_LIMIT = 56 * 1024 * 1024

ROW_TILE = 512
QKV_ROW_TILE = 1024
CONV_ROW_TILE = 1024
Q_TILE = 512
K_TILE = 512
KV_UNROLL = 2
ONES_ROWS = 16
PEN_ROWS = 16
CONV_HALO = 32
CONV_PAD = 8
CONV_ROWS = 64
NEG_BIG = -1e30
LOG2_E = math.log2(math.e)

_NT = (((1,), (1,)), ((), ()))


def _layer_norm(y, g, b):
    mu = jnp.mean(y, axis=-1, keepdims=True)
    d = y - mu
    var = jnp.mean(d * d, axis=-1, keepdims=True)
    return d * lax.rsqrt(var + LN_EPS) * g + b


def _rope_rows(t, cos, sin):
    pieces = []
    for g in range(D_MODEL // HEAD_DIM):
        t1 = t[g * HEAD_DIM:g * HEAD_DIM + HALF]
        t2 = t[g * HEAD_DIM + HALF:(g + 1) * HEAD_DIM]
        pieces.append(t1 * cos - t2 * sin)
        pieces.append(t1 * sin + t2 * cos)
    return jnp.concatenate(pieces, axis=0)


def _qkv_kernel(x_ref, pos_ref, inv_ref, wt_ref, qt_ref, k_ref, vt_ref):
    tm = x_ref.shape[1]
    xb = x_ref[0].astype(jnp.bfloat16)
    ang = inv_ref[...] * pos_ref[0]
    cos = jnp.cos(ang)
    sin = jnp.sin(ang)
    scale = HEAD_DIM ** -0.5 * LOG2_E

    qt = lax.dot_general(wt_ref[0:D_MODEL, :], xb, _NT,
                         preferred_element_type=jnp.float32)
    qt = _rope_rows(qt, cos * scale, sin * scale).astype(qt_ref.dtype)
    for j in range(tm // Q_TILE):
        qt_ref[0, j] = qt[:, j * Q_TILE:(j + 1) * Q_TILE]

    kt = lax.dot_general(wt_ref[D_MODEL:2 * D_MODEL, :], xb, _NT,
                         preferred_element_type=jnp.float32)
    k_ref[0] = _rope_rows(kt, cos, sin).T.astype(k_ref.dtype)

    vt = lax.dot_general(wt_ref[2 * D_MODEL:3 * D_MODEL, :], xb, _NT,
                         preferred_element_type=jnp.float32).astype(vt_ref.dtype)
    for j in range(tm // K_TILE):
        vt_ref[0, j] = vt[:, j * K_TILE:(j + 1) * K_TILE]


def _qkv_call(x, pos3, inv_col, wt_qkv):
    B, S, _ = x.shape
    tm = QKV_ROW_TILE
    const = lambda b, i: (0, 0)
    return pl.pallas_call(
        _qkv_kernel,
        grid=(B, S // tm),
        in_specs=[pl.BlockSpec((1, tm, D_MODEL), lambda b, i: (b, i, 0)),
                  pl.BlockSpec((1, 1, tm), lambda b, i: (b, 0, i)),
                  pl.BlockSpec((HALF, 1), const),
                  pl.BlockSpec((3 * D_MODEL, D_MODEL), const)],
        out_specs=[pl.BlockSpec((1, tm // Q_TILE, D_MODEL, Q_TILE), lambda b, i: (b, i, 0, 0)),
                   pl.BlockSpec((1, tm, D_MODEL), lambda b, i: (b, i, 0)),
                   pl.BlockSpec((1, tm // K_TILE, D_MODEL, K_TILE), lambda b, i: (b, i, 0, 0))],
        out_shape=[jax.ShapeDtypeStruct((B, S // Q_TILE, D_MODEL, Q_TILE), jnp.bfloat16),
                   jax.ShapeDtypeStruct((B, S, D_MODEL), jnp.bfloat16),
                   jax.ShapeDtypeStruct((B, S // K_TILE, D_MODEL, K_TILE), jnp.bfloat16)],
        compiler_params=pltpu.CompilerParams(
            dimension_semantics=("arbitrary", "arbitrary"), vmem_limit_bytes=VMEM_LIMIT),
        name="qkv_rope",
    )(x, pos3, inv_col, wt_qkv)


def _attn_schedule(nq):
    tiles_per_q = Q_TILE // K_TILE
    qs, ks = [], []
    for qi in range(nq):
        for kj in range((qi + 1) * tiles_per_q):
            qs.append(qi)
            ks.append(kj)
    return np.asarray(qs, np.int32), np.asarray(ks, np.int32)


def _attn_kernel(qtab_ref, ktab_ref, lam_ref, g_ref, qt_ref, k_ref, vt_ref, o_ref,
                 s_ref, m_ref, l_ref, acc_ref, *, n_steps, unroll):
    nq = qt_ref.shape[1]
    n_groups = n_steps // unroll
    chunks_per_k = K_TILE // CHUNK
    chunks_per_q = Q_TILE // CHUNK
    lp = lam_ref[...]
    lam = (jnp.exp(jnp.sum(lp[0:1] * lp[1:2], axis=-1, keepdims=True))
           - jnp.exp(jnp.sum(lp[2:3] * lp[3:4], axis=-1, keepdims=True))
           + LAMBDA_INIT)
    qry_chunk = lax.broadcasted_iota(jnp.int32, (1, Q_TILE), 1) // CHUNK
    pen_row = lax.broadcasted_iota(jnp.int32, (PEN_ROWS, Q_TILE), 0)
    key_row_chunk = lax.broadcasted_iota(jnp.int32, (K_TILE, LANES), 0) // CHUNK
    key_lane = lax.broadcasted_iota(jnp.int32, (K_TILE, LANES), 1)
    chunk_onehot = jnp.where(key_row_chunk == key_lane, 1.0, 0.0).astype(jnp.bfloat16)
    ones_rows = jnp.ones((ONES_ROWS, K_TILE), jnp.bfloat16)
    zeros = jnp.zeros((LANES, Q_TILE), jnp.bfloat16)

    m_ref[...] = jnp.full_like(m_ref, NEG_BIG)
    l_ref[...] = jnp.zeros_like(l_ref)
    acc_ref[...] = jnp.zeros_like(acc_ref)

    def scores(g, slot):
        for u in range(unroll):
            qi = qtab_ref[g * unroll + u]
            kj = ktab_ref[g * unroll + u]
            qt = qt_ref[0, qi]
            k0 = pl.multiple_of(kj * K_TILE, K_TILE)
            kt = jnp.concatenate([k_ref[0, pl.ds(k0, K_TILE), :], chunk_onehot], axis=1)
            hidden = (qry_chunk + qi * chunks_per_q) < (kj * chunks_per_k + pen_row)
            pen = jnp.where(hidden & (pen_row < chunks_per_k), NEG_BIG, 0.0).astype(jnp.bfloat16)
            pen = jnp.concatenate([pen, zeros[:LANES - PEN_ROWS]], axis=0)
            w0 = jnp.concatenate([qt[:HEAD_DIM], zeros[:HEAD_DIM], pen], axis=0)
            w1 = jnp.concatenate([zeros[:HEAD_DIM], qt[HEAD_DIM:], pen], axis=0)
            s_ref[slot, u, 0] = jnp.dot(kt, w0, preferred_element_type=jnp.float32)
            s_ref[slot, u, 1] = jnp.dot(kt, w1, preferred_element_type=jnp.float32)

    def softmax_pv(g, slot):
        for u in range(unroll):
            qi = qtab_ref[g * unroll + u]
            kj = ktab_ref[g * unroll + u]
            vt1 = jnp.concatenate([vt_ref[0, kj], ones_rows], axis=0)
            for c in range(2):
                s = s_ref[slot, u, c]
                m = m_ref[qi, c]
                m_new = jnp.maximum(m, jnp.max(s, axis=0, keepdims=True))
                p = jnp.exp2(s - m_new)
                alpha = jnp.exp2(m - m_new)
                pv = jnp.dot(vt1, p.astype(jnp.bfloat16), preferred_element_type=jnp.float32)
                m_ref[qi, c] = m_new
                l_ref[qi, c] = alpha * l_ref[qi, c] + pv[V_DIM:V_DIM + 1]
                acc_ref[qi, c] = alpha * acc_ref[qi, c] + pv[:V_DIM]

    scores(0, 0)

    def group_pair(j, carry):
        scores(2 * j + 1, 1)
        softmax_pv(2 * j, 0)
        scores(jnp.minimum(2 * j + 2, n_groups - 1), 0)
        softmax_pv(2 * j + 1, 1)
        return carry

    lax.fori_loop(0, n_groups // 2, group_pair, 0)

    def finish(qi, carry):
        o = acc_ref[qi, 0] * (1.0 / l_ref[qi, 0]) - acc_ref[qi, 1] * (lam / l_ref[qi, 1])
        y = o * lax.rsqrt(jnp.mean(o * o, axis=0, keepdims=True) + LN_EPS)
        q0 = pl.multiple_of(qi * Q_TILE, Q_TILE)
        o_ref[0, pl.ds(q0, Q_TILE), :] = y.T * (g_ref[...] * (1.0 - LAMBDA_INIT))
        return carry

    lax.fori_loop(0, nq, finish, 0, unroll=4)


def _attn_call(lam_params, subln_g, qt, k, vt):
    B, S, _ = k.shape
    nq = S // Q_TILE
    qtab, ktab = _attn_schedule(nq)
    n_steps = len(qtab)
    unroll = max(u for u in range(1, KV_UNROLL + 1) if n_steps % (2 * u) == 0)
    const = lambda b, h, *_: (0, 0)
    grid_spec = pltpu.PrefetchScalarGridSpec(
        num_scalar_prefetch=2,
        grid=(B, N_HEADS),
        in_specs=[pl.BlockSpec((4, HEAD_DIM), const),
                  pl.BlockSpec((1, V_DIM), const),
                  pl.BlockSpec((1, nq, V_DIM, Q_TILE), lambda b, h, *_: (b, 0, h, 0)),
                  pl.BlockSpec((1, S, LANES), lambda b, h, *_: (b, 0, h)),
                  pl.BlockSpec((1, S // K_TILE, V_DIM, K_TILE), lambda b, h, *_: (b, 0, h, 0))],
        out_specs=pl.BlockSpec((1, S, LANES), lambda b, h, *_: (b, 0, h)),
        scratch_shapes=[pltpu.VMEM((2, unroll, 2, K_TILE, Q_TILE), jnp.float32),
                        pltpu.VMEM((nq, 2, 1, Q_TILE), jnp.float32),
                        pltpu.VMEM((nq, 2, 1, Q_TILE), jnp.float32),
                        pltpu.VMEM((nq, 2, V_DIM, Q_TILE), jnp.float32)])
    return pl.pallas_call(
        functools.partial(_attn_kernel, n_steps=n_steps, unroll=unroll),
        grid_spec=grid_spec,
        out_shape=jax.ShapeDtypeStruct((B, S, D_MODEL), jnp.float32),
        compiler_params=pltpu.CompilerParams(
            dimension_semantics=("arbitrary", "arbitrary"), vmem_limit_bytes=VMEM_LIMIT),
        name="diff_attn",
    )(jnp.asarray(qtab), jnp.asarray(ktab), lam_params, subln_g, qt, k, vt)


def _depthwise_conv(ext_ref, dwk_ref, dwb_ref, out_ref, ts):
    base = CONV_HALO - (CONV_WIDTH - 1)
    groups = [[(j, (base + j) // 8) for j in range(CONV_WIDTH) if (base + j) % 8 == r]
              for r in range(8)]
    max_a = max(a for g in groups for _, a in g)
    rows = CONV_ROWS + CONV_PAD
    assert rows + 8 * max_a <= CONV_ROWS + CONV_HALO + CONV_PAD

    def col_body(c, carry):
        lanes = pl.ds(pl.multiple_of(c * LANES, LANES), LANES)
        kcol = dwk_ref[:, lanes]
        bias = dwb_ref[:, lanes]
        for rb in range(ts // CONV_ROWS):
            t0 = rb * CONV_ROWS
            e = ext_ref[t0:t0 + rows + 8 * max_a, lanes]
            w = None
            for r in range(7, -1, -1):
                v = None
                for j, a in groups[r]:
                    term = e[8 * a:8 * a + rows] * kcol[j:j + 1]
                    v = term if v is None else v + term
                if w is not None:
                    v = v + pltpu.roll(w, rows - 1, 0)
                w = v
            out_ref[t0:t0 + CONV_ROWS, lanes] = w[:CONV_ROWS] + bias
        return carry

    lax.fori_loop(0, D_MODEL // LANES, col_body, 0)


def _conv_kernel(x_ref, wga_ref, wgb_ref, bg_ref, dwk_ref, dwb_ref, lng_ref, lnb_ref,
                 wp_ref, bp_ref, o_ref, ext_ref, acc_ref):
    ts = x_ref.shape[1]

    @pl.when(pl.program_id(1) == 0)
    def _():
        ext_ref[0:CONV_HALO, :] = jnp.zeros((CONV_HALO, D_MODEL), jnp.float32)

    xb = x_ref[0].astype(jnp.bfloat16)
    ga = jnp.dot(xb, wga_ref[...], preferred_element_type=jnp.float32) + bg_ref[:, :D_MODEL]
    gb = jnp.dot(xb, wgb_ref[...], preferred_element_type=jnp.float32) + bg_ref[:, D_MODEL:]
    ext_ref[CONV_HALO:CONV_HALO + ts, :] = ga * jax.nn.sigmoid(gb)
    ext_ref[CONV_HALO + ts:, :] = jnp.zeros((CONV_PAD, D_MODEL), jnp.float32)
    _depthwise_conv(ext_ref, dwk_ref, dwb_ref, acc_ref, ts)
    ext_ref[0:CONV_HALO, :] = ext_ref[ts:ts + CONV_HALO, :]
    y = _layer_norm(acc_ref[...], lng_ref[...], lnb_ref[...])
    y = y * jax.nn.sigmoid(y)
    o_ref[0] = (jnp.dot(y.astype(jnp.bfloat16), wp_ref[...],
                        preferred_element_type=jnp.float32) + bp_ref[...])


def _conv_call(x, w_in_bf16, b_glu, dw_kernel, dw_bias, ln_g, ln_b, w_pw2, b_pw2):
    B, S, _ = x.shape
    ts = CONV_ROW_TILE
    row = lambda b, i: (b, i, 0)
    const = lambda b, i: (0, 0)
    vec = pl.BlockSpec((1, D_MODEL), const)
    glu_col = GLU_COL0 // D_MODEL
    return pl.pallas_call(
        _conv_kernel,
        grid=(B, S // ts),
        in_specs=[pl.BlockSpec((1, ts, D_MODEL), row),
                  pl.BlockSpec((D_MODEL, D_MODEL), lambda b, i: (0, glu_col)),
                  pl.BlockSpec((D_MODEL, D_MODEL), lambda b, i: (0, glu_col + 1)),
                  pl.BlockSpec((1, 2 * D_MODEL), const),
                  pl.BlockSpec((CONV_WIDTH, D_MODEL), const),
                  vec, vec, vec,
                  pl.BlockSpec((D_MODEL, D_MODEL), const),
                  vec],
        out_specs=pl.BlockSpec((1, ts, D_MODEL), row),
        out_shape=jax.ShapeDtypeStruct((B, S, D_MODEL), jnp.float32),
        scratch_shapes=[pltpu.VMEM((ts + CONV_HALO + CONV_PAD, D_MODEL), jnp.float32),
                        pltpu.VMEM((ts, D_MODEL), jnp.float32)],
        compiler_params=pltpu.CompilerParams(
            dimension_semantics=("arbitrary", "arbitrary"), vmem_limit_bytes=VMEM_LIMIT),
        name="conv_branch",
    )(x, w_in_bf16, w_in_bf16, b_glu, dw_kernel, dw_bias, ln_g, ln_b, w_pw2, b_pw2)


def _merge_mlp_kernel(x_ref, att_ref, conv_ref, wga_ref, wgc_ref, bgate_ref, wout_ref,
                      l1g_ref, l1b_ref, w1_ref, w2_ref, l2g_ref, l2b_ref, o_ref):
    half_rows = ROW_TILE // 2

    def merge(rows):
        x = x_ref[rows, :]
        xb = x.astype(jnp.bfloat16)
        g_att = jax.nn.sigmoid(jnp.dot(xb, wga_ref[...], preferred_element_type=jnp.float32)
                               + bgate_ref[:, :D_MODEL])
        g_conv = jax.nn.sigmoid(jnp.dot(xb, wgc_ref[...], preferred_element_type=jnp.float32)
                                + bgate_ref[:, D_MODEL:])
        mix = g_att * att_ref[rows, :] + g_conv * conv_ref[rows, :]
        mixed = jnp.dot(mix.astype(jnp.bfloat16), wout_ref[...], preferred_element_type=jnp.float32)
        return _layer_norm(DEEPNORM_ALPHA * x + mixed, l1g_ref[...], l1b_ref[...])

    def mlp(h1):
        hb = h1.astype(jnp.bfloat16)
        ff = None
        for c in range(D_FF // D_MODEL):
            a = jnp.dot(hb, w1_ref[:, c * D_MODEL:(c + 1) * D_MODEL],
                        preferred_element_type=jnp.float32)
            a = jnp.maximum(a, 0.0)
            a = (a * a).astype(jnp.bfloat16)
            part = jnp.dot(a, w2_ref[c * D_MODEL:(c + 1) * D_MODEL, :],
                           preferred_element_type=jnp.float32)
            ff = part if ff is None else ff + part
        return _layer_norm(DEEPNORM_ALPHA * h1 + ff, l2g_ref[...], l2b_ref[...])

    halves = [slice(h * half_rows, (h + 1) * half_rows) for h in range(2)]
    h1s = [merge(rows) for rows in halves]
    for rows, h1 in zip(halves, h1s):
        o_ref[rows, :] = mlp(h1)


def _merge_mlp_call(x2, att2, conv2, w_in_bf16, b_gate, w_out, ln1_g, ln1_b, w_ff1, w_ff2, ln2_g, ln2_b):
    T = x2.shape[0]
    tm = ROW_TILE
    row = lambda i: (i, 0)
    const = lambda i: (0, 0)
    wspec = lambda shape, index_map=const: pl.BlockSpec(shape, index_map, pipeline_mode=pl.Buffered(1))
    tile = pl.BlockSpec((tm, D_MODEL), row)
    vec = wspec((1, D_MODEL))
    gate_col = GATE_COL0 // D_MODEL
    return pl.pallas_call(
        _merge_mlp_kernel,
        grid=(T // tm,),
        in_specs=[tile, tile, tile,
                  wspec((D_MODEL, D_MODEL), lambda i: (0, gate_col)),
                  wspec((D_MODEL, D_MODEL), lambda i: (0, gate_col + 1)),
                  wspec((1, 2 * D_MODEL)),
                  wspec((D_MODEL, D_MODEL)), vec, vec,
                  wspec((D_MODEL, D_FF)), wspec((D_FF, D_MODEL)), vec, vec],
        out_specs=tile,
        out_shape=jax.ShapeDtypeStruct((T, D_MODEL), jnp.float32),
        compiler_params=pltpu.CompilerParams(
            dimension_semantics=("arbitrary",), vmem_limit_bytes=VMEM_LIMIT),
        name="merge_mlp",
    )(x2, att2, conv2, w_in_bf16, w_in_bf16, b_gate, w_out, ln1_g, ln1_b, w_ff1, w_ff2, ln2_g, ln2_b)


def kernel(x, positions, w_in, b_glu, b_gate, lambda_q1, lambda_k1, lambda_q2, lambda_k2, subln_g, dw_kernel, dw_bias, conv_ln_g, conv_ln_b, w_pw2, b_pw2, w_out, ln1_g, ln1_b, w_ff1, w_ff2, ln2_g, ln2_b):
    B, S, D = x.shape
    T = B * S
    bf16 = jnp.bfloat16
    inv_freq = ROPE_THETA ** (-jnp.arange(HALF, dtype=jnp.float32) * 2.0 / HEAD_DIM)
    inv_col = inv_freq[:, None]
    pos3 = positions.astype(jnp.float32).reshape(B, 1, S)
    x2 = x.reshape(T, D)

    l = 0
    w = w_in[l].astype(bf16)
    wt_qkv = w[:, :GLU_COL0].T
    row = lambda a: a[l][None, :]

    qt, k, vt = _qkv_call(x, pos3, inv_col, wt_qkv)
    lam_params = jnp.stack([lambda_q1[l], lambda_k1[l], lambda_q2[l], lambda_k2[l]])
    att = _attn_call(lam_params, row(subln_g), qt, k, vt)
    conv = _conv_call(x, w, row(b_glu), dw_kernel[l], row(dw_bias),
                      row(conv_ln_g), row(conv_ln_b), w_pw2[l].astype(bf16), row(b_pw2))
    out = _merge_mlp_call(x2, att.reshape(T, D), conv.reshape(T, D), w, row(b_gate),
                          w_out[l].astype(bf16), row(ln1_g), row(ln1_b),
                          w_ff1[l].astype(bf16), w_ff2[l].astype(bf16), row(ln2_g), row(ln2_b))
    return out.reshape(B, S, D)
```

```python
import functools
import math

import jax
import jax.numpy as jnp
import numpy as np
from jax import lax
from jax.experimental import pallas as pl
from jax.experimental.pallas import tpu as pltpu

D_MODEL = 1024
N_HEADS = 8
HEAD_DIM = 64
HALF = HEAD_DIM // 2
V_DIM = 2 * HEAD_DIM
CHUNK = 64
CONV_WIDTH = 31
D_FF = 4 * D_MODEL
ROPE_THETA = 10000.0
LN_EPS = 1e-5
DEPTH = 1
DEEPNORM_ALPHA = (2.0 * DEPTH) ** 0.25
LAMBDA_INIT = 0.8 - 0.6 * math.exp(-0.3 * 0)
GLU_COL0 = 3 * D_MODEL
GATE_COL0 = 5 * D_MODEL

LANES = 128
VMEM_LIMIT = 56 * 1024 * 1024

ROW_TILE = 512
QKV_ROW_TILE = 1024
CONV_ROW_TILE = 1024
Q_TILE = 512
K_TILE = 512
KV_UNROLL = 2
ONES_ROWS = 16
PEN_ROWS = 16
CONV_HALO = 32
CONV_PAD = 8
CONV_ROWS = 64
NEG_BIG = -1e30
LOG2_E = math.log2(math.e)

_NT = (((1,), (1,)), ((), ()))


def _layer_norm(y, g, b):
    mu = jnp.mean(y, axis=-1, keepdims=True)
    d = y - mu
    var = jnp.mean(d * d, axis=-1, keepdims=True)
    return d * lax.rsqrt(var + LN_EPS) * g + b


def _rope_rows(t, cos, sin):
    pieces = []
    for g in range(D_MODEL // HEAD_DIM):
        t1 = t[g * HEAD_DIM:g * HEAD_DIM + HALF]
        t2 = t[g * HEAD_DIM + HALF:(g + 1) * HEAD_DIM]
        pieces.append(t1 * cos - t2 * sin)
        pieces.append(t1 * sin + t2 * cos)
    return jnp.concatenate(pieces, axis=0)


def _qkv_kernel(x_ref, pos_ref, inv_ref, wt_ref, qt_ref, k_ref, vt_ref):
    tm = x_ref.shape[1]
    xb = x_ref[0].astype(jnp.bfloat16)
    ang = inv_ref[...] * pos_ref[0]
    cos = jnp.cos(ang)
    sin = jnp.sin(ang)
    scale = HEAD_DIM ** -0.5 * LOG2_E

    qt = lax.dot_general(wt_ref[0:D_MODEL, :], xb, _NT,
                         preferred_element_type=jnp.float32)
    qt = _rope_rows(qt, cos * scale, sin * scale).astype(qt_ref.dtype)
    for j in range(tm // Q_TILE):
        qt_ref[0, j] = qt[:, j * Q_TILE:(j + 1) * Q_TILE]

    kt = lax.dot_general(wt_ref[D_MODEL:2 * D_MODEL, :], xb, _NT,
                         preferred_element_type=jnp.float32)
    k_ref[0] = _rope_rows(kt, cos, sin).T.astype(k_ref.dtype)

    vt = lax.dot_general(wt_ref[2 * D_MODEL:3 * D_MODEL, :], xb, _NT,
                         preferred_element_type=jnp.float32).astype(vt_ref.dtype)
    for j in range(tm // K_TILE):
        vt_ref[0, j] = vt[:, j * K_TILE:(j + 1) * K_TILE]


def _qkv_call(x, pos3, inv_col, wt_qkv):
    B, S, _ = x.shape
    tm = QKV_ROW_TILE
    const = lambda b, i: (0, 0)
    return pl.pallas_call(
        _qkv_kernel,
        grid=(B, S // tm),
        in_specs=[pl.BlockSpec((1, tm, D_MODEL), lambda b, i: (b, i, 0)),
                  pl.BlockSpec((1, 1, tm), lambda b, i: (b, 0, i)),
                  pl.BlockSpec((HALF, 1), const),
                  pl.BlockSpec((3 * D_MODEL, D_MODEL), const)],
        out_specs=[pl.BlockSpec((1, tm // Q_TILE, D_MODEL, Q_TILE), lambda b, i: (b, i, 0, 0)),
                   pl.BlockSpec((1, tm, D_MODEL), lambda b, i: (b, i, 0)),
                   pl.BlockSpec((1, tm // K_TILE, D_MODEL, K_TILE), lambda b, i: (b, i, 0, 0))],
        out_shape=[jax.ShapeDtypeStruct((B, S // Q_TILE, D_MODEL, Q_TILE), jnp.bfloat16),
                   jax.ShapeDtypeStruct((B, S, D_MODEL), jnp.bfloat16),
                   jax.ShapeDtypeStruct((B, S // K_TILE, D_MODEL, K_TILE), jnp.bfloat16)],
        compiler_params=pltpu.CompilerParams(
            dimension_semantics=("arbitrary", "arbitrary"), vmem_limit_bytes=VMEM_LIMIT),
        name="qkv_rope",
    )(x, pos3, inv_col, wt_qkv)


def _attn_schedule(nq):
    tiles_per_q = Q_TILE // K_TILE
    qs, ks = [], []
    for qi in range(nq):
        for kj in range((qi + 1) * tiles_per_q):
            qs.append(qi)
            ks.append(kj)
    return np.asarray(qs, np.int32), np.asarray(ks, np.int32)


def _attn_kernel(qtab_ref, ktab_ref, lam_ref, g_ref, qt_ref, k_ref, vt_ref, o_ref,
                 s_ref, m_ref, l_ref, acc_ref, *, n_steps, unroll):
    nq = qt_ref.shape[1]
    n_groups = n_steps // unroll
    chunks_per_k = K_TILE // CHUNK
    chunks_per_q = Q_TILE // CHUNK
    lp = lam_ref[...]
    lam = (jnp.exp(jnp.sum(lp[0:1] * lp[1:2], axis=-1, keepdims=True))
           - jnp.exp(jnp.sum(lp[2:3] * lp[3:4], axis=-1, keepdims=True))
           + LAMBDA_INIT)
    qry_chunk = lax.broadcasted_iota(jnp.int32, (1, Q_TILE), 1) // CHUNK
    pen_row = lax.broadcasted_iota(jnp.int32, (PEN_ROWS, Q_TILE), 0)
    key_row_chunk = lax.broadcasted_iota(jnp.int32, (K_TILE, LANES), 0) // CHUNK
    key_lane = lax.broadcasted_iota(jnp.int32, (K_TILE, LANES), 1)
    chunk_onehot = jnp.where(key_row_chunk == key_lane, 1.0, 0.0).astype(jnp.bfloat16)
    ones_rows = jnp.ones((ONES_ROWS, K_TILE), jnp.bfloat16)
    zeros = jnp.zeros((LANES, Q_TILE), jnp.bfloat16)

    m_ref[...] = jnp.full_like(m_ref, NEG_BIG)
    l_ref[...] = jnp.zeros_like(l_ref)
    acc_ref[...] = jnp.zeros_like(acc_ref)

    def scores(g, slot):
        for u in range(unroll):
            qi = qtab_ref[g * unroll + u]
            kj = ktab_ref[g * unroll + u]
            qt = qt_ref[0, qi]
            k0 = pl.multiple_of(kj * K_TILE, K_TILE)
            kt = jnp.concatenate([k_ref[0, pl.ds(k0, K_TILE), :], chunk_onehot], axis=1)
            hidden = (qry_chunk + qi * chunks_per_q) < (kj * chunks_per_k + pen_row)
            pen = jnp.where(hidden & (pen_row < chunks_per_k), NEG_BIG, 0.0).astype(jnp.bfloat16)
            pen = jnp.concatenate([pen, zeros[:LANES - PEN_ROWS]], axis=0)
            w0 = jnp.concatenate([qt[:HEAD_DIM], zeros[:HEAD_DIM], pen], axis=0)
            w1 = jnp.concatenate([zeros[:HEAD_DIM], qt[HEAD_DIM:], pen], axis=0)
            s_ref[slot, u, 0] = jnp.dot(kt, w0, preferred_element_type=jnp.float32)
            s_ref[slot, u, 1] = jnp.dot(kt, w1, preferred_element_type=jnp.float32)

    def softmax_pv(g, slot):
        for u in range(unroll):
            qi = qtab_ref[g * unroll + u]
            kj = ktab_ref[g * unroll + u]
            vt1 = jnp.concatenate([vt_ref[0, kj], ones_rows], axis=0)
            for c in range(2):
                s = s_ref[slot, u, c]
                m = m_ref[qi, c]
                m_new = jnp.maximum(m, jnp.max(s, axis=0, keepdims=True))
                p = jnp.exp2(s - m_new)
                alpha = jnp.exp2(m - m_new)
                pv = jnp.dot(vt1, p.astype(jnp.bfloat16), preferred_element_type=jnp.float32)
                m_ref[qi, c] = m_new
                l_ref[qi, c] = alpha * l_ref[qi, c] + pv[V_DIM:V_DIM + 1]
                acc_ref[qi, c] = alpha * acc_ref[qi, c] + pv[:V_DIM]

    scores(0, 0)

    def group_pair(j, carry):
        scores(2 * j + 1, 1)
        softmax_pv(2 * j, 0)
        scores(jnp.minimum(2 * j + 2, n_groups - 1), 0)
        softmax_pv(2 * j + 1, 1)
        return carry

    lax.fori_loop(0, n_groups // 2, group_pair, 0)

    def finish(qi, carry):
        o = acc_ref[qi, 0] * (1.0 / l_ref[qi, 0]) - acc_ref[qi, 1] * (lam / l_ref[qi, 1])
        y = o * lax.rsqrt(jnp.mean(o * o, axis=0, keepdims=True) + LN_EPS)
        q0 = pl.multiple_of(qi * Q_TILE, Q_TILE)
        o_ref[0, pl.ds(q0, Q_TILE), :] = y.T * (g_ref[...] * (1.0 - LAMBDA_INIT))
        return carry

    lax.fori_loop(0, nq, finish, 0, unroll=True)


def _attn_call(lam_params, subln_g, qt, k, vt):
    B, S, _ = k.shape
    nq = S // Q_TILE
    qtab, ktab = _attn_schedule(nq)
    n_steps = len(qtab)
    unroll = max(u for u in range(1, KV_UNROLL + 1) if n_steps % (2 * u) == 0)
    const = lambda b, h, *_: (0, 0)
    grid_spec = pltpu.PrefetchScalarGridSpec(
        num_scalar_prefetch=2,
        grid=(B, N_HEADS),
        in_specs=[pl.BlockSpec((4, HEAD_DIM), const),
                  pl.BlockSpec((1, V_DIM), const),
                  pl.BlockSpec((1, nq, V_DIM, Q_TILE), lambda b, h, *_: (b, 0, h, 0)),
                  pl.BlockSpec((1, S, LANES), lambda b, h, *_: (b, 0, h)),
                  pl.BlockSpec((1, S // K_TILE, V_DIM, K_TILE), lambda b, h, *_: (b, 0, h, 0))],
        out_specs=pl.BlockSpec((1, S, LANES), lambda b, h, *_: (b, 0, h)),
        scratch_shapes=[pltpu.VMEM((2, unroll, 2, K_TILE, Q_TILE), jnp.float32),
                        pltpu.VMEM((nq, 2, 1, Q_TILE), jnp.float32),
                        pltpu.VMEM((nq, 2, 1, Q_TILE), jnp.float32),
                        pltpu.VMEM((nq, 2, V_DIM, Q_TILE), jnp.float32)])
    return pl.pallas_call(
        functools.partial(_attn_kernel, n_steps=n_steps, unroll=unroll),
        grid_spec=grid_spec,
        out_shape=jax.ShapeDtypeStruct((B, S, D_MODEL), jnp.float32),
        compiler_params=pltpu.CompilerParams(
            dimension_semantics=("arbitrary", "arbitrary"), vmem_limit_bytes=VMEM_LIMIT),
        name="diff_attn",
    )(jnp.asarray(qtab), jnp.asarray(ktab), lam_params, subln_g, qt, k, vt)


def _depthwise_conv(ext_ref, dwk_ref, dwb_ref, out_ref, ts):
    base = CONV_HALO - (CONV_WIDTH - 1)
    groups = [[(j, (base + j) // 8) for j in range(CONV_WIDTH) if (base + j) % 8 == r]
              for r in range(8)]
    max_a = max(a for g in groups for _, a in g)
    rows = CONV_ROWS + CONV_PAD
    assert rows + 8 * max_a <= CONV_ROWS + CONV_HALO + CONV_PAD

    def col_body(c, carry):
        lanes = pl.ds(pl.multiple_of(c * LANES, LANES), LANES)
        kcol = dwk_ref[:, lanes]
        bias = dwb_ref[:, lanes]
        for rb in range(ts // CONV_ROWS):
            t0 = rb * CONV_ROWS
            e = ext_ref[t0:t0 + rows + 8 * max_a, lanes]
            w = None
            for r in range(7, -1, -1):
                v = None
                for j, a in groups[r]:
                    term = e[8 * a:8 * a + rows] * kcol[j:j + 1]
                    v = term if v is None else v + term
                if w is not None:
                    v = v + pltpu.roll(w, rows - 1, 0)
                w = v
            out_ref[t0:t0 + CONV_ROWS, lanes] = w[:CONV_ROWS] + bias
        return carry

    lax.fori_loop(0, D_MODEL // LANES, col_body, 0)


def _conv_kernel(x_ref, wga_ref, wgb_ref, bg_ref, dwk_ref, dwb_ref, lng_ref, lnb_ref,
                 wp_ref, bp_ref, o_ref, ext_ref, acc_ref):
    ts = x_ref.shape[1]

    @pl.when(pl.program_id(1) == 0)
    def _():
        ext_ref[0:CONV_HALO, :] = jnp.zeros((CONV_HALO, D_MODEL), jnp.float32)

    half = ts // 2
    for h in range(2):
        xb = x_ref[0, h * half:(h + 1) * half, :].astype(jnp.bfloat16)
        ga = jnp.dot(xb, wga_ref[...], preferred_element_type=jnp.float32) + bg_ref[:, :D_MODEL]
        gb = jnp.dot(xb, wgb_ref[...], preferred_element_type=jnp.float32) + bg_ref[:, D_MODEL:]
        ext_ref[CONV_HALO + h * half:CONV_HALO + (h + 1) * half, :] = ga * jax.nn.sigmoid(gb)
    ext_ref[CONV_HALO + ts:, :] = jnp.zeros((CONV_PAD, D_MODEL), jnp.float32)
    _depthwise_conv(ext_ref, dwk_ref, dwb_ref, acc_ref, ts)
    ext_ref[0:CONV_HALO, :] = ext_ref[ts:ts + CONV_HALO, :]
    for h in range(2):
        rows = slice(h * half, (h + 1) * half)
        y = _layer_norm(acc_ref[rows, :], lng_ref[...], lnb_ref[...])
        y = y * jax.nn.sigmoid(y)
        o_ref[0, rows, :] = (jnp.dot(y.astype(jnp.bfloat16), wp_ref[...],
                                     preferred_element_type=jnp.float32) + bp_ref[...])


def _conv_call(x, w_in_bf16, b_glu, dw_kernel, dw_bias, ln_g, ln_b, w_pw2, b_pw2):
    B, S, _ = x.shape
    ts = CONV_ROW_TILE
    row = lambda b, i: (b, i, 0)
    const = lambda b, i: (0, 0)
    vec = pl.BlockSpec((1, D_MODEL), const)
    glu_col = GLU_COL0 // D_MODEL
    return pl.pallas_call(
        _conv_kernel,
        grid=(B, S // ts),
        in_specs=[pl.BlockSpec((1, ts, D_MODEL), row),
                  pl.BlockSpec((D_MODEL, D_MODEL), lambda b, i: (0, glu_col)),
                  pl.BlockSpec((D_MODEL, D_MODEL), lambda b, i: (0, glu_col + 1)),
                  pl.BlockSpec((1, 2 * D_MODEL), const),
                  pl.BlockSpec((CONV_WIDTH, D_MODEL), const),
                  vec, vec, vec,
                  pl.BlockSpec((D_MODEL, D_MODEL), const),
                  vec],
        out_specs=pl.BlockSpec((1, ts, D_MODEL), row),
        out_shape=jax.ShapeDtypeStruct((B, S, D_MODEL), jnp.float32),
        scratch_shapes=[pltpu.VMEM((ts + CONV_HALO + CONV_PAD, D_MODEL), jnp.float32),
                        pltpu.VMEM((ts, D_MODEL), jnp.float32)],
        compiler_params=pltpu.CompilerParams(
            dimension_semantics=("arbitrary", "arbitrary"), vmem_limit_bytes=VMEM_LIMIT),
        name="conv_branch",
    )(x, w_in_bf16, w_in_bf16, b_glu, dw_kernel, dw_bias, ln_g, ln_b, w_pw2, b_pw2)


def _merge_mlp_kernel(x_ref, att_ref, conv_ref, wga_ref, wgc_ref, bgate_ref, wout_ref,
                      l1g_ref, l1b_ref, w1_ref, w2_ref, l2g_ref, l2b_ref, o_ref):
    half_rows = ROW_TILE // 2

    def merge(rows):
        x = x_ref[rows, :]
        xb = x.astype(jnp.bfloat16)
        g_att = jax.nn.sigmoid(jnp.dot(xb, wga_ref[...], preferred_element_type=jnp.float32)
                               + bgate_ref[:, :D_MODEL])
        g_conv = jax.nn.sigmoid(jnp.dot(xb, wgc_ref[...], preferred_element_type=jnp.float32)
                                + bgate_ref[:, D_MODEL:])
        mix = g_att * att_ref[rows, :] + g_conv * conv_ref[rows, :]
        mixed = jnp.dot(mix.astype(jnp.bfloat16), wout_ref[...], preferred_element_type=jnp.float32)
        return _layer_norm(DEEPNORM_ALPHA * x + mixed, l1g_ref[...], l1b_ref[...])

    def mlp(h1):
        hb = h1.astype(jnp.bfloat16)
        ff = None
        for c in range(D_FF // D_MODEL):
            a = jnp.dot(hb, w1_ref[:, c * D_MODEL:(c + 1) * D_MODEL],
                        preferred_element_type=jnp.float32)
            a = jnp.maximum(a, 0.0)
            a = (a * a).astype(jnp.bfloat16)
            part = jnp.dot(a, w2_ref[c * D_MODEL:(c + 1) * D_MODEL, :],
                           preferred_element_type=jnp.float32)
            ff = part if ff is None else ff + part
        return _layer_norm(DEEPNORM_ALPHA * h1 + ff, l2g_ref[...], l2b_ref[...])

    halves = [slice(h * half_rows, (h + 1) * half_rows) for h in range(2)]
    h1s = [merge(rows) for rows in halves]
    for rows, h1 in zip(halves, h1s):
        o_ref[rows, :] = mlp(h1)


def _merge_mlp_call(x2, att2, conv2, w_in_bf16, b_gate, w_out, ln1_g, ln1_b, w_ff1, w_ff2, ln2_g, ln2_b):
    T = x2.shape[0]
    tm = ROW_TILE
    row = lambda i: (i, 0)
    const = lambda i: (0, 0)
    wspec = lambda shape, index_map=const: pl.BlockSpec(shape, index_map, pipeline_mode=pl.Buffered(1))
    tile = pl.BlockSpec((tm, D_MODEL), row)
    vec = wspec((1, D_MODEL))
    gate_col = GATE_COL0 // D_MODEL
    return pl.pallas_call(
        _merge_mlp_kernel,
        grid=(T // tm,),
        in_specs=[tile, tile, tile,
                  wspec((D_MODEL, D_MODEL), lambda i: (0, gate_col)),
                  wspec((D_MODEL, D_MODEL), lambda i: (0, gate_col + 1)),
                  wspec((1, 2 * D_MODEL)),
                  wspec((D_MODEL, D_MODEL)), vec, vec,
                  wspec((D_MODEL, D_FF)), wspec((D_FF, D_MODEL)), vec, vec],
        out_specs=tile,
        out_shape=jax.ShapeDtypeStruct((T, D_MODEL), jnp.float32),
        compiler_params=pltpu.CompilerParams(
            dimension_semantics=("arbitrary",), vmem_limit_bytes=VMEM_LIMIT),
        name="merge_mlp",
    )(x2, att2, conv2, w_in_bf16, w_in_bf16, b_gate, w_out, ln1_g, ln1_b, w_ff1, w_ff2, ln2_g, ln2_b)


def kernel(x, positions, w_in, b_glu, b_gate, lambda_q1, lambda_k1, lambda_q2, lambda_k2, subln_g, dw_kernel, dw_bias, conv_ln_g, conv_ln_b, w_pw2, b_pw2, w_out, ln1_g, ln1_b, w_ff1, w_ff2, ln2_g, ln2_b):
    B, S, D = x.shape
    T = B * S
    bf16 = jnp.bfloat16
    inv_freq = ROPE_THETA ** (-jnp.arange(HALF, dtype=jnp.float32) * 2.0 / HEAD_DIM)
    inv_col = inv_freq[:, None]
    pos3 = positions.astype(jnp.float32).reshape(B, 1, S)
    x2 = x.reshape(T, D)

    l = 0
    w = w_in[l].astype(bf16)
    wt_qkv = w[:, :GLU_COL0].T
    row = lambda a: a[l][None, :]

    qt, k, vt = _qkv_call(x, pos3, inv_col, wt_qkv)
    lam_params = jnp.stack([lambda_q1[l], lambda_k1[l], lambda_q2[l], lambda_k2[l]])
    att = _attn_call(lam_params, row(subln_g), qt, k, vt)
    conv = _conv_call(x, w, row(b_glu), dw_kernel[l], row(dw_bias),
                      row(conv_ln_g), row(conv_ln_b), w_pw2[l].astype(bf16), row(b_pw2))
    out = _merge_mlp_call(x2, att.reshape(T, D), conv.reshape(T, D), w, row(b_gate),
                          w_out[l].astype(bf16), row(ln1_g), row(ln1_b),
                          w_ff1[l].astype(bf16), w_ff2[l].astype(bf16), row(ln2_g), row(ln2_b))
    return out.reshape(B, S, D)
```

```python
import functools
import math

import jax
import jax.numpy as jnp
import numpy as np
from jax import lax
from jax.experimental import pallas as pl
from jax.experimental.pallas import tpu as pltpu

D_MODEL = 1024
N_HEADS = 8
HEAD_DIM = 64
HALF = HEAD_DIM // 2
V_DIM = 2 * HEAD_DIM
CHUNK = 64
CONV_WIDTH = 31
D_FF = 4 * D_MODEL
ROPE_THETA = 10000.0
LN_EPS = 1e-5
DEPTH = 1
DEEPNORM_ALPHA = (2.0 * DEPTH) ** 0.25
LAMBDA_INIT = 0.8 - 0.6 * math.exp(-0.3 * 0)
GLU_COL0 = 3 * D_MODEL
GATE_COL0 = 5 * D_MODEL

LANES = 128
VMEM_LIMIT = 56 * 1024 * 1024

ROW_TILE = 512
QKV_ROW_TILE = 1024
CONV_ROW_TILE = 1024
Q_TILE = 512
K_TILE = 512
KV_UNROLL = 2
ONES_ROWS = 16
PEN_ROWS = 16
CONV_HALO = 32
CONV_PAD = 8
CONV_ROWS = 32
NEG_BIG = -1e30
LOG2_E = math.log2(math.e)

_NT = (((1,), (1,)), ((), ()))


def _layer_norm(y, g, b):
    mu = jnp.mean(y, axis=-1, keepdims=True)
    d = y - mu
    var = jnp.mean(d * d, axis=-1, keepdims=True)
    return d * lax.rsqrt(var + LN_EPS) * g + b


def _rope_rows(t, cos, sin):
    pieces = []
    for g in range(D_MODEL // HEAD_DIM):
        t1 = t[g * HEAD_DIM:g * HEAD_DIM + HALF]
        t2 = t[g * HEAD_DIM + HALF:(g + 1) * HEAD_DIM]
        pieces.append(t1 * cos - t2 * sin)
        pieces.append(t1 * sin + t2 * cos)
    return jnp.concatenate(pieces, axis=0)


def _qkv_kernel(x_ref, pos_ref, inv_ref, wt_ref, qt_ref, k_ref, vt_ref):
    tm = x_ref.shape[1]
    xb = x_ref[0].astype(jnp.bfloat16)
    ang = inv_ref[...] * pos_ref[0]
    cos = jnp.cos(ang)
    sin = jnp.sin(ang)
    scale = HEAD_DIM ** -0.5 * LOG2_E

    qt = lax.dot_general(wt_ref[0:D_MODEL, :], xb, _NT,
                         preferred_element_type=jnp.float32)
    qt = _rope_rows(qt, cos * scale, sin * scale).astype(qt_ref.dtype)
    for j in range(tm // Q_TILE):
        qt_ref[0, j] = qt[:, j * Q_TILE:(j + 1) * Q_TILE]

    kt = lax.dot_general(wt_ref[D_MODEL:2 * D_MODEL, :], xb, _NT,
                         preferred_element_type=jnp.float32)
    k_ref[0] = _rope_rows(kt, cos, sin).T.astype(k_ref.dtype)

    vt = lax.dot_general(wt_ref[2 * D_MODEL:3 * D_MODEL, :], xb, _NT,
                         preferred_element_type=jnp.float32).astype(vt_ref.dtype)
    for j in range(tm // K_TILE):
        vt_ref[0, j] = vt[:, j * K_TILE:(j + 1) * K_TILE]


def _qkv_call(x, pos3, inv_col, wt_qkv):
    B, S, _ = x.shape
    tm = QKV_ROW_TILE
    const = lambda b, i: (0, 0)
    return pl.pallas_call(
        _qkv_kernel,
        grid=(B, S // tm),
        in_specs=[pl.BlockSpec((1, tm, D_MODEL), lambda b, i: (b, i, 0)),
                  pl.BlockSpec((1, 1, tm), lambda b, i: (b, 0, i)),
                  pl.BlockSpec((HALF, 1), const),
                  pl.BlockSpec((3 * D_MODEL, D_MODEL), const)],
        out_specs=[pl.BlockSpec((1, tm // Q_TILE, D_MODEL, Q_TILE), lambda b, i: (b, i, 0, 0)),
                   pl.BlockSpec((1, tm, D_MODEL), lambda b, i: (b, i, 0)),
                   pl.BlockSpec((1, tm // K_TILE, D_MODEL, K_TILE), lambda b, i: (b, i, 0, 0))],
        out_shape=[jax.ShapeDtypeStruct((B, S // Q_TILE, D_MODEL, Q_TILE), jnp.bfloat16),
                   jax.ShapeDtypeStruct((B, S, D_MODEL), jnp.bfloat16),
                   jax.ShapeDtypeStruct((B, S // K_TILE, D_MODEL, K_TILE), jnp.bfloat16)],
        compiler_params=pltpu.CompilerParams(
            dimension_semantics=("arbitrary", "arbitrary"), vmem_limit_bytes=VMEM_LIMIT),
        name="qkv_rope",
    )(x, pos3, inv_col, wt_qkv)


def _attn_schedule(nq):
    tiles_per_q = Q_TILE // K_TILE
    qs, ks = [], []
    for qi in range(nq):
        for kj in range((qi + 1) * tiles_per_q):
            qs.append(qi)
            ks.append(kj)
    return np.asarray(qs, np.int32), np.asarray(ks, np.int32)


def _attn_kernel(qtab_ref, ktab_ref, lam_ref, g_ref, qt_ref, k_ref, vt_ref, o_ref,
                 s_ref, m_ref, l_ref, acc_ref, *, n_steps, unroll):
    nq = qt_ref.shape[1]
    n_groups = n_steps // unroll
    chunks_per_k = K_TILE // CHUNK
    chunks_per_q = Q_TILE // CHUNK
    lp = lam_ref[...]
    lam = (jnp.exp(jnp.sum(lp[0:1] * lp[1:2], axis=-1, keepdims=True))
           - jnp.exp(jnp.sum(lp[2:3] * lp[3:4], axis=-1, keepdims=True))
           + LAMBDA_INIT)
    qry_chunk = lax.broadcasted_iota(jnp.int32, (1, Q_TILE), 1) // CHUNK
    pen_row = lax.broadcasted_iota(jnp.int32, (PEN_ROWS, Q_TILE), 0)
    key_row_chunk = lax.broadcasted_iota(jnp.int32, (K_TILE, LANES), 0) // CHUNK
    key_lane = lax.broadcasted_iota(jnp.int32, (K_TILE, LANES), 1)
    chunk_onehot = jnp.where(key_row_chunk == key_lane, 1.0, 0.0).astype(jnp.bfloat16)
    ones_rows = jnp.ones((ONES_ROWS, K_TILE), jnp.bfloat16)
    zeros = jnp.zeros((LANES, Q_TILE), jnp.bfloat16)

    m_ref[...] = jnp.full_like(m_ref, NEG_BIG)
    l_ref[...] = jnp.zeros_like(l_ref)
    acc_ref[...] = jnp.zeros_like(acc_ref)

    def scores(g, slot):
        for u in range(unroll):
            qi = qtab_ref[g * unroll + u]
            kj = ktab_ref[g * unroll + u]
            qt = qt_ref[0, qi]
            k0 = pl.multiple_of(kj * K_TILE, K_TILE)
            kt = jnp.concatenate([k_ref[0, pl.ds(k0, K_TILE), :], chunk_onehot], axis=1)
            hidden = (qry_chunk + qi * chunks_per_q) < (kj * chunks_per_k + pen_row)
            pen = jnp.where(hidden & (pen_row < chunks_per_k), NEG_BIG, 0.0).astype(jnp.bfloat16)
            pen = jnp.concatenate([pen, zeros[:LANES - PEN_ROWS]], axis=0)
            w0 = jnp.concatenate([qt[:HEAD_DIM], zeros[:HEAD_DIM], pen], axis=0)
            w1 = jnp.concatenate([zeros[:HEAD_DIM], qt[HEAD_DIM:], pen], axis=0)
            s_ref[slot, u, 0] = jnp.dot(kt, w0, preferred_element_type=jnp.float32)
            s_ref[slot, u, 1] = jnp.dot(kt, w1, preferred_element_type=jnp.float32)

    def softmax_pv(g, slot):
        for u in range(unroll):
            qi = qtab_ref[g * unroll + u]
            kj = ktab_ref[g * unroll + u]
            vt1 = jnp.concatenate([vt_ref[0, kj], ones_rows], axis=0)
            for c in range(2):
                s = s_ref[slot, u, c]
                m = m_ref[qi, c]
                m_new = jnp.maximum(m, jnp.max(s, axis=0, keepdims=True))
                p = jnp.exp2(s - m_new)
                alpha = jnp.exp2(m - m_new)
                pv = jnp.dot(vt1, p.astype(jnp.bfloat16), preferred_element_type=jnp.float32)
                m_ref[qi, c] = m_new
                l_ref[qi, c] = alpha * l_ref[qi, c] + pv[V_DIM:V_DIM + 1]
                acc_ref[qi, c] = alpha * acc_ref[qi, c] + pv[:V_DIM]

    scores(0, 0)

    def group_pair(j, carry):
        scores(2 * j + 1, 1)
        softmax_pv(2 * j, 0)
        scores(jnp.minimum(2 * j + 2, n_groups - 1), 0)
        softmax_pv(2 * j + 1, 1)
        return carry

    lax.fori_loop(0, n_groups // 2, group_pair, 0)

    def finish(qi, carry):
        o = acc_ref[qi, 0] * (1.0 / l_ref[qi, 0]) - acc_ref[qi, 1] * (lam / l_ref[qi, 1])
        y = o * lax.rsqrt(jnp.mean(o * o, axis=0, keepdims=True) + LN_EPS)
        q0 = pl.multiple_of(qi * Q_TILE, Q_TILE)
        o_ref[0, pl.ds(q0, Q_TILE), :] = y.T * (g_ref[...] * (1.0 - LAMBDA_INIT))
        return carry

    lax.fori_loop(0, nq, finish, 0, unroll=True)


def _attn_call(lam_params, subln_g, qt, k, vt):
    B, S, _ = k.shape
    nq = S // Q_TILE
    qtab, ktab = _attn_schedule(nq)
    n_steps = len(qtab)
    unroll = max(u for u in range(1, KV_UNROLL + 1) if n_steps % (2 * u) == 0)
    const = lambda b, h, *_: (0, 0)
    grid_spec = pltpu.PrefetchScalarGridSpec(
        num_scalar_prefetch=2,
        grid=(B, N_HEADS),
        in_specs=[pl.BlockSpec((4, HEAD_DIM), const),
                  pl.BlockSpec((1, V_DIM), const),
                  pl.BlockSpec((1, nq, V_DIM, Q_TILE), lambda b, h, *_: (b, 0, h, 0)),
                  pl.BlockSpec((1, S, LANES), lambda b, h, *_: (b, 0, h)),
                  pl.BlockSpec((1, S // K_TILE, V_DIM, K_TILE), lambda b, h, *_: (b, 0, h, 0))],
        out_specs=pl.BlockSpec((1, S, LANES), lambda b, h, *_: (b, 0, h)),
        scratch_shapes=[pltpu.VMEM((2, unroll, 2, K_TILE, Q_TILE), jnp.float32),
                        pltpu.VMEM((nq, 2, 1, Q_TILE), jnp.float32),
                        pltpu.VMEM((nq, 2, 1, Q_TILE), jnp.float32),
                        pltpu.VMEM((nq, 2, V_DIM, Q_TILE), jnp.float32)])
    return pl.pallas_call(
        functools.partial(_attn_kernel, n_steps=n_steps, unroll=unroll),
        grid_spec=grid_spec,
        out_shape=jax.ShapeDtypeStruct((B, S, D_MODEL), jnp.float32),
        compiler_params=pltpu.CompilerParams(
            dimension_semantics=("arbitrary", "arbitrary"), vmem_limit_bytes=VMEM_LIMIT),
        name="diff_attn",
    )(jnp.asarray(qtab), jnp.asarray(ktab), lam_params, subln_g, qt, k, vt)


def _depthwise_conv(ext_ref, dwk_ref, dwb_ref, out_ref, ts):
    base = CONV_HALO - (CONV_WIDTH - 1)
    groups = [[(j, (base + j) // 8) for j in range(CONV_WIDTH) if (base + j) % 8 == r]
              for r in range(8)]
    max_a = max(a for g in groups for _, a in g)
    rows = CONV_ROWS + CONV_PAD
    assert rows + 8 * max_a <= CONV_ROWS + CONV_HALO + CONV_PAD

    def col_body(c, carry):
        lanes = pl.ds(pl.multiple_of(c * LANES, LANES), LANES)
        kcol = dwk_ref[:, lanes]
        bias = dwb_ref[:, lanes]
        for rb in range(ts // CONV_ROWS):
            t0 = rb * CONV_ROWS
            e = ext_ref[t0:t0 + rows + 8 * max_a, lanes]
            w = None
            for r in range(7, -1, -1):
                v = None
                for j, a in groups[r]:
                    term = e[8 * a:8 * a + rows] * kcol[j:j + 1]
                    v = term if v is None else v + term
                if w is not None:
                    v = v + pltpu.roll(w, rows - 1, 0)
                w = v
            out_ref[t0:t0 + CONV_ROWS, lanes] = w[:CONV_ROWS] + bias
        return carry

    lax.fori_loop(0, D_MODEL // LANES, col_body, 0)


def _conv_kernel(x_ref, wga_ref, wgb_ref, bg_ref, dwk_ref, dwb_ref, lng_ref, lnb_ref,
                 wp_ref, bp_ref, o_ref, ext_ref, acc_ref):
    ts = x_ref.shape[1]

    @pl.when(pl.program_id(1) == 0)
    def _():
        ext_ref[0:CONV_HALO, :] = jnp.zeros((CONV_HALO, D_MODEL), jnp.float32)

    half = ts // 2
    for h in range(2):
        xb = x_ref[0, h * half:(h + 1) * half, :].astype(jnp.bfloat16)
        ga = jnp.dot(xb, wga_ref[...], preferred_element_type=jnp.float32) + bg_ref[:, :D_MODEL]
        gb = jnp.dot(xb, wgb_ref[...], preferred_element_type=jnp.float32) + bg_ref[:, D_MODEL:]
        ext_ref[CONV_HALO + h * half:CONV_HALO + (h + 1) * half, :] = ga * jax.nn.sigmoid(gb)
    ext_ref[CONV_HALO + ts:, :] = jnp.zeros((CONV_PAD, D_MODEL), jnp.float32)
    _depthwise_conv(ext_ref, dwk_ref, dwb_ref, acc_ref, ts)
    ext_ref[0:CONV_HALO, :] = ext_ref[ts:ts + CONV_HALO, :]
    for h in range(2):
        rows = slice(h * half, (h + 1) * half)
        y = _layer_norm(acc_ref[rows, :], lng_ref[...], lnb_ref[...])
        y = y * jax.nn.sigmoid(y)
        o_ref[0, rows, :] = (jnp.dot(y.astype(jnp.bfloat16), wp_ref[...],
                                     preferred_element_type=jnp.float32) + bp_ref[...])


def _conv_call(x, w_in_bf16, b_glu, dw_kernel, dw_bias, ln_g, ln_b, w_pw2, b_pw2):
    B, S, _ = x.shape
    ts = CONV_ROW_TILE
    row = lambda b, i: (b, i, 0)
    const = lambda b, i: (0, 0)
    vec = pl.BlockSpec((1, D_MODEL), const)
    glu_col = GLU_COL0 // D_MODEL
    return pl.pallas_call(
        _conv_kernel,
        grid=(B, S // ts),
        in_specs=[pl.BlockSpec((1, ts, D_MODEL), row),
                  pl.BlockSpec((D_MODEL, D_MODEL), lambda b, i: (0, glu_col)),
                  pl.BlockSpec((D_MODEL, D_MODEL), lambda b, i: (0, glu_col + 1)),
                  pl.BlockSpec((1, 2 * D_MODEL), const),
                  pl.BlockSpec((CONV_WIDTH, D_MODEL), const),
                  vec, vec, vec,
                  pl.BlockSpec((D_MODEL, D_MODEL), const),
                  vec],
        out_specs=pl.BlockSpec((1, ts, D_MODEL), row),
        out_shape=jax.ShapeDtypeStruct((B, S, D_MODEL), jnp.float32),
        scratch_shapes=[pltpu.VMEM((ts + CONV_HALO + CONV_PAD, D_MODEL), jnp.float32),
                        pltpu.VMEM((ts, D_MODEL), jnp.float32)],
        compiler_params=pltpu.CompilerParams(
            dimension_semantics=("arbitrary", "arbitrary"), vmem_limit_bytes=VMEM_LIMIT),
        name="conv_branch",
    )(x, w_in_bf16, w_in_bf16, b_glu, dw_kernel, dw_bias, ln_g, ln_b, w_pw2, b_pw2)


def _merge_mlp_kernel(x_ref, att_ref, conv_ref, wga_ref, wgc_ref, bgate_ref, wout_ref,
                      l1g_ref, l1b_ref, w1_ref, w2_ref, l2g_ref, l2b_ref, o_ref):
    half_rows = ROW_TILE // 2

    def merge(rows):
        x = x_ref[rows, :]
        xb = x.astype(jnp.bfloat16)
        g_att = jax.nn.sigmoid(jnp.dot(xb, wga_ref[...], preferred_element_type=jnp.float32)
                               + bgate_ref[:, :D_MODEL])
        g_conv = jax.nn.sigmoid(jnp.dot(xb, wgc_ref[...], preferred_element_type=jnp.float32)
                                + bgate_ref[:, D_MODEL:])
        mix = g_att * att_ref[rows, :] + g_conv * conv_ref[rows, :]
        mixed = jnp.dot(mix.astype(jnp.bfloat16), wout_ref[...], preferred_element_type=jnp.float32)
        return _layer_norm(DEEPNORM_ALPHA * x + mixed, l1g_ref[...], l1b_ref[...])

    def mlp(h1):
        hb = h1.astype(jnp.bfloat16)
        ff = None
        for c in range(D_FF // D_MODEL):
            a = jnp.dot(hb, w1_ref[:, c * D_MODEL:(c + 1) * D_MODEL],
                        preferred_element_type=jnp.float32)
            a = jnp.maximum(a, 0.0)
            a = (a * a).astype(jnp.bfloat16)
            part = jnp.dot(a, w2_ref[c * D_MODEL:(c + 1) * D_MODEL, :],
                           preferred_element_type=jnp.float32)
            ff = part if ff is None else ff + part
        return _layer_norm(DEEPNORM_ALPHA * h1 + ff, l2g_ref[...], l2b_ref[...])

    halves = [slice(h * half_rows, (h + 1) * half_rows) for h in range(2)]
    h1s = [merge(rows) for rows in halves]
    for rows, h1 in zip(halves, h1s):
        o_ref[rows, :] = mlp(h1)


def _merge_mlp_call(x2, att2, conv2, w_in_bf16, b_gate, w_out, ln1_g, ln1_b, w_ff1, w_ff2, ln2_g, ln2_b):
    T = x2.shape[0]
    tm = ROW_TILE
    row = lambda i: (i, 0)
    const = lambda i: (0, 0)
    wspec = lambda shape, index_map=const: pl.BlockSpec(shape, index_map, pipeline_mode=pl.Buffered(1))
    tile = pl.BlockSpec((tm, D_MODEL), row)
    vec = wspec((1, D_MODEL))
    gate_col = GATE_COL0 // D_MODEL
    return pl.pallas_call(
        _merge_mlp_kernel,
        grid=(T // tm,),
        in_specs=[tile, tile, tile,
                  wspec((D_MODEL, D_MODEL), lambda i: (0, gate_col)),
                  wspec((D_MODEL, D_MODEL), lambda i: (0, gate_col + 1)),
                  wspec((1, 2 * D_MODEL)),
                  wspec((D_MODEL, D_MODEL)), vec, vec,
                  wspec((D_MODEL, D_FF)), wspec((D_FF, D_MODEL)), vec, vec],
        out_specs=tile,
        out_shape=jax.ShapeDtypeStruct((T, D_MODEL), jnp.float32),
        compiler_params=pltpu.CompilerParams(
            dimension_semantics=("arbitrary",), vmem_limit_bytes=VMEM_LIMIT),
        name="merge_mlp",
    )(x2, att2, conv2, w_in_bf16, w_in_bf16, b_gate, w_out, ln1_g, ln1_b, w_ff1, w_ff2, ln2_g, ln2_b)


def kernel(x, positions, w_in, b_glu, b_gate, lambda_q1, lambda_k1, lambda_q2, lambda_k2, subln_g, dw_kernel, dw_bias, conv_ln_g, conv_ln_b, w_pw2, b_pw2, w_out, ln1_g, ln1_b, w_ff1, w_ff2, ln2_g, ln2_b):
    B, S, D = x.shape
    T = B * S
    assert D == D_MODEL and w_in.shape == (DEPTH, D_MODEL, GATE_COL0 + 2 * D_MODEL)
    assert S % QKV_ROW_TILE == 0 and S % CONV_ROW_TILE == 0 and T % ROW_TILE == 0
    assert QKV_ROW_TILE % Q_TILE == 0 and QKV_ROW_TILE % K_TILE == 0 and Q_TILE % K_TILE == 0
    bf16 = jnp.bfloat16
    inv_freq = ROPE_THETA ** (-jnp.arange(HALF, dtype=jnp.float32) * 2.0 / HEAD_DIM)
    inv_col = inv_freq[:, None]
    pos3 = positions.astype(jnp.float32).reshape(B, 1, S)
    x2 = x.reshape(T, D)

    l = 0
    w = w_in[l].astype(bf16)
    wt_qkv = w[:, :GLU_COL0].T
    row = lambda a: a[l][None, :]

    qt, k, vt = _qkv_call(x, pos3, inv_col, wt_qkv)
    lam_params = jnp.stack([lambda_q1[l], lambda_k1[l], lambda_q2[l], lambda_k2[l]])
    att = _attn_call(lam_params, row(subln_g), qt, k, vt)
    conv = _conv_call(x, w, row(b_glu), dw_kernel[l], row(dw_bias),
                      row(conv_ln_g), row(conv_ln_b), w_pw2[l].astype(bf16), row(b_pw2))
    out = _merge_mlp_call(x2, att.reshape(T, D), conv.reshape(T, D), w, row(b_gate),
                          w_out[l].astype(bf16), row(ln1_g), row(ln1_b),
                          w_ff1[l].astype(bf16), w_ff2[l].astype(bf16), row(ln2_g), row(ln2_b))
    return out.reshape(B, S, D)
```

```python
import functools
import math

import jax
import jax.numpy as jnp
import numpy as np
from jax import lax
from jax.experimental import pallas as pl
from jax.experimental.pallas import tpu as pltpu

D_MODEL = 1024
N_HEADS = 8
HEAD_DIM = 64
HALF = HEAD_DIM // 2
V_DIM = 2 * HEAD_DIM
CHUNK = 64
CONV_WIDTH = 31
D_FF = 4 * D_MODEL
ROPE_THETA = 10000.0
LN_EPS = 1e-5
DEPTH = 1
DEEPNORM_ALPHA = (2.0 * DEPTH) ** 0.25
LAMBDA_INIT = 0.8 - 0.6 * math.exp(-0.3 * 0)
GLU_COL0 = 3 * D_MODEL
GATE_COL0 = 5 * D_MODEL

LANES = 128
VMEM_LIMIT = 56 * 1024 * 1024

ROW_TILE = 512
QKV_ROW_TILE = 1024
CONV_ROW_TILE = 1024
Q_TILE = 512
K_TILE = 512
KV_UNROLL = 2
ONES_ROWS = 16
PEN_ROWS = 16
CONV_HALO = 32
CONV_PAD = 8
CONV_ROWS = 32
NEG_BIG = -1e30
LOG2_E = math.log2(math.e)

_NT = (((1,), (1,)), ((), ()))


def _layer_norm(y, g, b):
    mu = jnp.mean(y, axis=-1, keepdims=True)
    d = y - mu
    var = jnp.mean(d * d, axis=-1, keepdims=True)
    return d * lax.rsqrt(var + LN_EPS) * g + b


def _rope_rows(t, cos, sin):
    pieces = []
    for g in range(D_MODEL // HEAD_DIM):
        t1 = t[g * HEAD_DIM:g * HEAD_DIM + HALF]
        t2 = t[g * HEAD_DIM + HALF:(g + 1) * HEAD_DIM]
        pieces.append(t1 * cos - t2 * sin)
        pieces.append(t1 * sin + t2 * cos)
    return jnp.concatenate(pieces, axis=0)


def _qkv_kernel(x_ref, pos_ref, inv_ref, wt_ref, qt_ref, k_ref, vt_ref):
    tm = x_ref.shape[1]
    xb = x_ref[0].astype(jnp.bfloat16)
    ang = inv_ref[...] * pos_ref[0]
    cos = jnp.cos(ang)
    sin = jnp.sin(ang)
    scale = HEAD_DIM ** -0.5 * LOG2_E

    qt = lax.dot_general(wt_ref[0:D_MODEL, :], xb, _NT,
                         preferred_element_type=jnp.float32)
    qt = _rope_rows(qt, cos * scale, sin * scale).astype(qt_ref.dtype)
    for j in range(tm // Q_TILE):
        qt_ref[0, j] = qt[:, j * Q_TILE:(j + 1) * Q_TILE]

    kt = lax.dot_general(wt_ref[D_MODEL:2 * D_MODEL, :], xb, _NT,
                         preferred_element_type=jnp.float32)
    k_ref[0] = _rope_rows(kt, cos, sin).T.astype(k_ref.dtype)

    vt = lax.dot_general(wt_ref[2 * D_MODEL:3 * D_MODEL, :], xb, _NT,
                         preferred_element_type=jnp.float32).astype(vt_ref.dtype)
    for j in range(tm // K_TILE):
        vt_ref[0, j] = vt[:, j * K_TILE:(j + 1) * K_TILE]


def _qkv_call(x, pos3, inv_col, wt_qkv):
    B, S, _ = x.shape
    tm = QKV_ROW_TILE
    const = lambda b, i: (0, 0)
    return pl.pallas_call(
        _qkv_kernel,
        grid=(B, S // tm),
        in_specs=[pl.BlockSpec((1, tm, D_MODEL), lambda b, i: (b, i, 0)),
                  pl.BlockSpec((1, 1, tm), lambda b, i: (b, 0, i)),
                  pl.BlockSpec((HALF, 1), const),
                  pl.BlockSpec((3 * D_MODEL, D_MODEL), const)],
        out_specs=[pl.BlockSpec((1, tm // Q_TILE, D_MODEL, Q_TILE), lambda b, i: (b, i, 0, 0)),
                   pl.BlockSpec((1, tm, D_MODEL), lambda b, i: (b, i, 0)),
                   pl.BlockSpec((1, tm // K_TILE, D_MODEL, K_TILE), lambda b, i: (b, i, 0, 0))],
        out_shape=[jax.ShapeDtypeStruct((B, S // Q_TILE, D_MODEL, Q_TILE), jnp.bfloat16),
                   jax.ShapeDtypeStruct((B, S, D_MODEL), jnp.bfloat16),
                   jax.ShapeDtypeStruct((B, S // K_TILE, D_MODEL, K_TILE), jnp.bfloat16)],
        compiler_params=pltpu.CompilerParams(
            dimension_semantics=("arbitrary", "arbitrary"), vmem_limit_bytes=VMEM_LIMIT),
        name="qkv_rope",
    )(x, pos3, inv_col, wt_qkv)


def _attn_schedule(nq):
    tiles_per_q = Q_TILE // K_TILE
    qs, ks = [], []
    for qi in range(nq):
        for kj in range((qi + 1) * tiles_per_q):
            qs.append(qi)
            ks.append(kj)
    return np.asarray(qs, np.int32), np.asarray(ks, np.int32)


def _attn_kernel(qtab_ref, ktab_ref, lam_ref, g_ref, qt_ref, k_ref, vt_ref, o_ref,
                 s_ref, m_ref, l_ref, acc_ref, *, n_steps, unroll):
    nq = qt_ref.shape[1]
    n_groups = n_steps // unroll
    chunks_per_k = K_TILE // CHUNK
    chunks_per_q = Q_TILE // CHUNK
    lp = lam_ref[...]
    lam = (jnp.exp(jnp.sum(lp[0:1] * lp[1:2], axis=-1, keepdims=True))
           - jnp.exp(jnp.sum(lp[2:3] * lp[3:4], axis=-1, keepdims=True))
           + LAMBDA_INIT)
    qry_chunk = lax.broadcasted_iota(jnp.int32, (1, Q_TILE), 1) // CHUNK
    pen_row = lax.broadcasted_iota(jnp.int32, (PEN_ROWS, Q_TILE), 0)
    key_row_chunk = lax.broadcasted_iota(jnp.int32, (K_TILE, LANES), 0) // CHUNK
    key_lane = lax.broadcasted_iota(jnp.int32, (K_TILE, LANES), 1)
    chunk_onehot = jnp.where(key_row_chunk == key_lane, 1.0, 0.0).astype(jnp.bfloat16)
    ones_rows = jnp.ones((ONES_ROWS, K_TILE), jnp.bfloat16)
    zeros = jnp.zeros((LANES, Q_TILE), jnp.bfloat16)

    m_ref[...] = jnp.full_like(m_ref, NEG_BIG)
    l_ref[...] = jnp.zeros_like(l_ref)
    acc_ref[...] = jnp.zeros_like(acc_ref)

    def scores(g, slot):
        for u in range(unroll):
            qi = qtab_ref[g * unroll + u]
            kj = ktab_ref[g * unroll + u]
            qt = qt_ref[0, qi]
            k0 = pl.multiple_of(kj * K_TILE, K_TILE)
            kt = jnp.concatenate([k_ref[0, pl.ds(k0, K_TILE), :], chunk_onehot], axis=1)
            hidden = (qry_chunk + qi * chunks_per_q) < (kj * chunks_per_k + pen_row)
            pen = jnp.where(hidden & (pen_row < chunks_per_k), NEG_BIG, 0.0).astype(jnp.bfloat16)
            pen = jnp.concatenate([pen, zeros[:LANES - PEN_ROWS]], axis=0)
            w0 = jnp.concatenate([qt[:HEAD_DIM], zeros[:HEAD_DIM], pen], axis=0)
            w1 = jnp.concatenate([zeros[:HEAD_DIM], qt[HEAD_DIM:], pen], axis=0)
            s_ref[slot, u, 0] = jnp.dot(kt, w0, preferred_element_type=jnp.float32)
            s_ref[slot, u, 1] = jnp.dot(kt, w1, preferred_element_type=jnp.float32)

    def softmax_pv(g, slot):
        for u in range(unroll):
            qi = qtab_ref[g * unroll + u]
            kj = ktab_ref[g * unroll + u]
            vt1 = jnp.concatenate([vt_ref[0, kj], ones_rows], axis=0)
            for c in range(2):
                s = s_ref[slot, u, c]
                m = m_ref[qi, c]
                m_new = jnp.maximum(m, jnp.max(s, axis=0, keepdims=True))
                p = jnp.exp2(s - m_new)
                alpha = jnp.exp2(m - m_new)
                pv = jnp.dot(vt1, p.astype(jnp.bfloat16), preferred_element_type=jnp.float32)
                m_ref[qi, c] = m_new
                l_ref[qi, c] = alpha * l_ref[qi, c] + pv[V_DIM:V_DIM + 1]
                acc_ref[qi, c] = alpha * acc_ref[qi, c] + pv[:V_DIM]

    scores(0, 0)

    def group_pair(j, carry):
        scores(2 * j + 1, 1)
        softmax_pv(2 * j, 0)
        scores(2 * j + 2, 0)
        softmax_pv(2 * j + 1, 1)
        return carry

    lax.fori_loop(0, n_groups // 2 - 1, group_pair, 0)
    scores(n_groups - 1, 1)
    softmax_pv(n_groups - 2, 0)
    softmax_pv(n_groups - 1, 1)

    def finish(qi, carry):
        o = acc_ref[qi, 0] * (1.0 / l_ref[qi, 0]) - acc_ref[qi, 1] * (lam / l_ref[qi, 1])
        y = o * lax.rsqrt(jnp.mean(o * o, axis=0, keepdims=True) + LN_EPS)
        q0 = pl.multiple_of(qi * Q_TILE, Q_TILE)
        o_ref[0, pl.ds(q0, Q_TILE), :] = y.T * (g_ref[...] * (1.0 - LAMBDA_INIT))
        return carry

    lax.fori_loop(0, nq, finish, 0, unroll=True)


def _attn_call(lam_params, subln_g, qt, k, vt):
    B, S, _ = k.shape
    nq = S // Q_TILE
    qtab, ktab = _attn_schedule(nq)
    n_steps = len(qtab)
    unroll = max(u for u in range(1, KV_UNROLL + 1) if n_steps % (2 * u) == 0)
    const = lambda b, h, *_: (0, 0)
    grid_spec = pltpu.PrefetchScalarGridSpec(
        num_scalar_prefetch=2,
        grid=(B, N_HEADS),
        in_specs=[pl.BlockSpec((4, HEAD_DIM), const),
                  pl.BlockSpec((1, V_DIM), const),
                  pl.BlockSpec((1, nq, V_DIM, Q_TILE), lambda b, h, *_: (b, 0, h, 0)),
                  pl.BlockSpec((1, S, LANES), lambda b, h, *_: (b, 0, h)),
                  pl.BlockSpec((1, S // K_TILE, V_DIM, K_TILE), lambda b, h, *_: (b, 0, h, 0))],
        out_specs=pl.BlockSpec((1, S, LANES), lambda b, h, *_: (b, 0, h)),
        scratch_shapes=[pltpu.VMEM((2, unroll, 2, K_TILE, Q_TILE), jnp.float32),
                        pltpu.VMEM((nq, 2, 1, Q_TILE), jnp.float32),
                        pltpu.VMEM((nq, 2, 1, Q_TILE), jnp.float32),
                        pltpu.VMEM((nq, 2, V_DIM, Q_TILE), jnp.float32)])
    return pl.pallas_call(
        functools.partial(_attn_kernel, n_steps=n_steps, unroll=unroll),
        grid_spec=grid_spec,
        out_shape=jax.ShapeDtypeStruct((B, S, D_MODEL), jnp.float32),
        compiler_params=pltpu.CompilerParams(
            dimension_semantics=("arbitrary", "arbitrary"), vmem_limit_bytes=VMEM_LIMIT),
        name="diff_attn",
    )(jnp.asarray(qtab), jnp.asarray(ktab), lam_params, subln_g, qt, k, vt)


def _depthwise_conv(ext_ref, dwk_ref, dwb_ref, out_ref, ts):
    base = CONV_HALO - (CONV_WIDTH - 1)
    groups = [[(j, (base + j) // 8) for j in range(CONV_WIDTH) if (base + j) % 8 == r]
              for r in range(8)]
    max_a = max(a for g in groups for _, a in g)
    rows = CONV_ROWS + CONV_PAD
    assert rows + 8 * max_a <= CONV_ROWS + CONV_HALO + CONV_PAD

    def col_body(c, carry):
        lanes = pl.ds(pl.multiple_of(c * LANES, LANES), LANES)
        kcol = dwk_ref[:, lanes]
        bias = dwb_ref[:, lanes]
        for rb in range(ts // CONV_ROWS):
            t0 = rb * CONV_ROWS
            e = ext_ref[t0:t0 + rows + 8 * max_a, lanes]
            w = None
            for r in range(7, -1, -1):
                v = None
                for j, a in groups[r]:
                    term = e[8 * a:8 * a + rows] * kcol[j:j + 1]
                    v = term if v is None else v + term
                if w is not None:
                    v = v + pltpu.roll(w, rows - 1, 0)
                w = v
            out_ref[t0:t0 + CONV_ROWS, lanes] = w[:CONV_ROWS] + bias
        return carry

    lax.fori_loop(0, D_MODEL // LANES, col_body, 0)


def _conv_kernel(x_ref, wga_ref, wgb_ref, bg_ref, dwk_ref, dwb_ref, lng_ref, lnb_ref,
                 wp_ref, bp_ref, o_ref, ext_ref, acc_ref):
    ts = x_ref.shape[1]

    @pl.when(pl.program_id(1) == 0)
    def _():
        ext_ref[0:CONV_HALO, :] = jnp.zeros((CONV_HALO, D_MODEL), jnp.float32)

    half = ts // 2
    for h in range(2):
        xb = x_ref[0, h * half:(h + 1) * half, :].astype(jnp.bfloat16)
        ga = jnp.dot(xb, wga_ref[...], preferred_element_type=jnp.float32) + bg_ref[:, :D_MODEL]
        gb = jnp.dot(xb, wgb_ref[...], preferred_element_type=jnp.float32) + bg_ref[:, D_MODEL:]
        ext_ref[CONV_HALO + h * half:CONV_HALO + (h + 1) * half, :] = ga * jax.nn.sigmoid(gb)
    ext_ref[CONV_HALO + ts:, :] = jnp.zeros((CONV_PAD, D_MODEL), jnp.float32)
    _depthwise_conv(ext_ref, dwk_ref, dwb_ref, acc_ref, ts)
    ext_ref[0:CONV_HALO, :] = ext_ref[ts:ts + CONV_HALO, :]
    for h in range(2):
        rows = slice(h * half, (h + 1) * half)
        y = _layer_norm(acc_ref[rows, :], lng_ref[...], lnb_ref[...])
        y = y * jax.nn.sigmoid(y)
        o_ref[0, rows, :] = (jnp.dot(y.astype(jnp.bfloat16), wp_ref[...],
                                     preferred_element_type=jnp.float32) + bp_ref[...])


def _conv_call(x, w_in_bf16, b_glu, dw_kernel, dw_bias, ln_g, ln_b, w_pw2, b_pw2):
    B, S, _ = x.shape
    ts = CONV_ROW_TILE
    row = lambda b, i: (b, i, 0)
    const = lambda b, i: (0, 0)
    vec = pl.BlockSpec((1, D_MODEL), const)
    glu_col = GLU_COL0 // D_MODEL
    return pl.pallas_call(
        _conv_kernel,
        grid=(B, S // ts),
        in_specs=[pl.BlockSpec((1, ts, D_MODEL), row),
                  pl.BlockSpec((D_MODEL, D_MODEL), lambda b, i: (0, glu_col)),
                  pl.BlockSpec((D_MODEL, D_MODEL), lambda b, i: (0, glu_col + 1)),
                  pl.BlockSpec((1, 2 * D_MODEL), const),
                  pl.BlockSpec((CONV_WIDTH, D_MODEL), const),
                  vec, vec, vec,
                  pl.BlockSpec((D_MODEL, D_MODEL), const),
                  vec],
        out_specs=pl.BlockSpec((1, ts, D_MODEL), row),
        out_shape=jax.ShapeDtypeStruct((B, S, D_MODEL), jnp.float32),
        scratch_shapes=[pltpu.VMEM((ts + CONV_HALO + CONV_PAD, D_MODEL), jnp.float32),
                        pltpu.VMEM((ts, D_MODEL), jnp.float32)],
        compiler_params=pltpu.CompilerParams(
            dimension_semantics=("arbitrary", "arbitrary"), vmem_limit_bytes=VMEM_LIMIT),
        name="conv_branch",
    )(x, w_in_bf16, w_in_bf16, b_glu, dw_kernel, dw_bias, ln_g, ln_b, w_pw2, b_pw2)


def _merge_mlp_kernel(x_ref, att_ref, conv_ref, wga_ref, wgc_ref, bgate_ref, wout_ref,
                      l1g_ref, l1b_ref, w1_ref, w2_ref, l2g_ref, l2b_ref, o_ref):
    half_rows = ROW_TILE // 2

    def merge(rows):
        x = x_ref[rows, :]
        xb = x.astype(jnp.bfloat16)
        g_att = jax.nn.sigmoid(jnp.dot(xb, wga_ref[...], preferred_element_type=jnp.float32)
                               + bgate_ref[:, :D_MODEL])
        g_conv = jax.nn.sigmoid(jnp.dot(xb, wgc_ref[...], preferred_element_type=jnp.float32)
                                + bgate_ref[:, D_MODEL:])
        mix = g_att * att_ref[rows, :] + g_conv * conv_ref[rows, :]
        mixed = jnp.dot(mix.astype(jnp.bfloat16), wout_ref[...], preferred_element_type=jnp.float32)
        return _layer_norm(DEEPNORM_ALPHA * x + mixed, l1g_ref[...], l1b_ref[...])

    def mlp(h1):
        hb = h1.astype(jnp.bfloat16)
        ff = None
        for c in range(D_FF // D_MODEL):
            a = jnp.dot(hb, w1_ref[:, c * D_MODEL:(c + 1) * D_MODEL],
                        preferred_element_type=jnp.float32)
            a = jnp.maximum(a, 0.0)
            a = (a * a).astype(jnp.bfloat16)
            part = jnp.dot(a, w2_ref[c * D_MODEL:(c + 1) * D_MODEL, :],
                           preferred_element_type=jnp.float32)
            ff = part if ff is None else ff + part
        return _layer_norm(DEEPNORM_ALPHA * h1 + ff, l2g_ref[...], l2b_ref[...])

    halves = [slice(h * half_rows, (h + 1) * half_rows) for h in range(2)]
    h1s = [merge(rows) for rows in halves]
    for rows, h1 in zip(halves, h1s):
        o_ref[rows, :] = mlp(h1)


def _merge_mlp_call(x2, att2, conv2, w_in_bf16, b_gate, w_out, ln1_g, ln1_b, w_ff1, w_ff2, ln2_g, ln2_b):
    T = x2.shape[0]
    tm = ROW_TILE
    row = lambda i: (i, 0)
    const = lambda i: (0, 0)
    wspec = lambda shape, index_map=const: pl.BlockSpec(shape, index_map, pipeline_mode=pl.Buffered(1))
    tile = pl.BlockSpec((tm, D_MODEL), row)
    vec = wspec((1, D_MODEL))
    gate_col = GATE_COL0 // D_MODEL
    return pl.pallas_call(
        _merge_mlp_kernel,
        grid=(T // tm,),
        in_specs=[tile, tile, tile,
                  wspec((D_MODEL, D_MODEL), lambda i: (0, gate_col)),
                  wspec((D_MODEL, D_MODEL), lambda i: (0, gate_col + 1)),
                  wspec((1, 2 * D_MODEL)),
                  wspec((D_MODEL, D_MODEL)), vec, vec,
                  wspec((D_MODEL, D_FF)), wspec((D_FF, D_MODEL)), vec, vec],
        out_specs=tile,
        out_shape=jax.ShapeDtypeStruct((T, D_MODEL), jnp.float32),
        compiler_params=pltpu.CompilerParams(
            dimension_semantics=("arbitrary",), vmem_limit_bytes=VMEM_LIMIT),
        name="merge_mlp",
    )(x2, att2, conv2, w_in_bf16, w_in_bf16, b_gate, w_out, ln1_g, ln1_b, w_ff1, w_ff2, ln2_g, ln2_b)


def kernel(x, positions, w_in, b_glu, b_gate, lambda_q1, lambda_k1, lambda_q2, lambda_k2, subln_g, dw_kernel, dw_bias, conv_ln_g, conv_ln_b, w_pw2, b_pw2, w_out, ln1_g, ln1_b, w_ff1, w_ff2, ln2_g, ln2_b):
    B, S, D = x.shape
    T = B * S
    assert D == D_MODEL and w_in.shape == (DEPTH, D_MODEL, GATE_COL0 + 2 * D_MODEL)
    assert S % QKV_ROW_TILE == 0 and S % CONV_ROW_TILE == 0 and T % ROW_TILE == 0
    assert QKV_ROW_TILE % Q_TILE == 0 and QKV_ROW_TILE % K_TILE == 0 and Q_TILE % K_TILE == 0
    bf16 = jnp.bfloat16
    inv_freq = ROPE_THETA ** (-jnp.arange(HALF, dtype=jnp.float32) * 2.0 / HEAD_DIM)
    inv_col = inv_freq[:, None]
    pos3 = positions.astype(jnp.float32).reshape(B, 1, S)
    x2 = x.reshape(T, D)

    l = 0
    w = w_in[l].astype(bf16)
    wt_qkv = w[:, :GLU_COL0].T
    row = lambda a: a[l][None, :]

    qt, k, vt = _qkv_call(x, pos3, inv_col, wt_qkv)
    lam_params = jnp.stack([lambda_q1[l], lambda_k1[l], lambda_q2[l], lambda_k2[l]])
    att = _attn_call(lam_params, row(subln_g), qt, k, vt)
    conv = _conv_call(x, w, row(b_glu), dw_kernel[l], row(dw_bias),
                      row(conv_ln_g), row(conv_ln_b), w_pw2[l].astype(bf16), row(b_pw2))
    out = _merge_mlp_call(x2, att.reshape(T, D), conv.reshape(T, D), w, row(b_gate),
                          w_out[l].astype(bf16), row(ln1_g), row(ln1_b),
                          w_ff1[l].astype(bf16), w_ff2[l].astype(bf16), row(ln2_g), row(ln2_b))
    return out.reshape(B, S, D)
```

```python
import functools
import math

import jax
import jax.numpy as jnp
import numpy as np
from jax import lax
from jax.experimental import pallas as pl
from jax.experimental.pallas import tpu as pltpu

D_MODEL = 1024
N_HEADS = 8
HEAD_DIM = 64
HALF = HEAD_DIM // 2
V_DIM = 2 * HEAD_DIM
CHUNK = 64
CONV_WIDTH = 31
D_FF = 4 * D_MODEL
ROPE_THETA = 10000.0
LN_EPS = 1e-5
DEPTH = 1
DEEPNORM_ALPHA = (2.0 * DEPTH) ** 0.25
LAMBDA_INIT = 0.8 - 0.6 * math.exp(-0.3 * 0)
GLU_COL0 = 3 * D_MODEL
GATE_COL0 = 5 * D_MODEL

LANES = 128
VMEM_LIMIT = 56 * 1024 * 1024

ROW_TILE = 512
QKV_ROW_TILE = 1024
CONV_ROW_TILE = 1024
Q_TILE = 512
K_TILE = 512
KV_UNROLL = 3
ONES_ROWS = 16
PEN_ROWS = 16
CONV_HALO = 32
CONV_PAD = 8
CONV_ROWS = 32
NEG_BIG = -1e30
LOG2_E = math.log2(math.e)

_NT = (((1,), (1,)), ((), ()))


def _layer_norm(y, g, b):
    mu = jnp.mean(y, axis=-1, keepdims=True)
    d = y - mu
    var = jnp.mean(d * d, axis=-1, keepdims=True)
    return d * lax.rsqrt(var + LN_EPS) * g + b


def _rope_rows(t, cos, sin):
    pieces = []
    for g in range(D_MODEL // HEAD_DIM):
        t1 = t[g * HEAD_DIM:g * HEAD_DIM + HALF]
        t2 = t[g * HEAD_DIM + HALF:(g + 1) * HEAD_DIM]
        pieces.append(t1 * cos - t2 * sin)
        pieces.append(t1 * sin + t2 * cos)
    return jnp.concatenate(pieces, axis=0)


def _qkv_kernel(x_ref, pos_ref, inv_ref, wt_ref, qt_ref, k_ref, vt_ref):
    tm = x_ref.shape[1]
    xb = x_ref[0].astype(jnp.bfloat16)
    ang = inv_ref[...] * pos_ref[0]
    cos = jnp.cos(ang)
    sin = jnp.sin(ang)
    scale = HEAD_DIM ** -0.5 * LOG2_E

    qt = lax.dot_general(wt_ref[0:D_MODEL, :], xb, _NT,
                         preferred_element_type=jnp.float32)
    qt = _rope_rows(qt, cos * scale, sin * scale).astype(qt_ref.dtype)
    for j in range(tm // Q_TILE):
        qt_ref[0, j] = qt[:, j * Q_TILE:(j + 1) * Q_TILE]

    kt = lax.dot_general(wt_ref[D_MODEL:2 * D_MODEL, :], xb, _NT,
                         preferred_element_type=jnp.float32)
    k_ref[0] = _rope_rows(kt, cos, sin).T.astype(k_ref.dtype)

    vt = lax.dot_general(wt_ref[2 * D_MODEL:3 * D_MODEL, :], xb, _NT,
                         preferred_element_type=jnp.float32).astype(vt_ref.dtype)
    for j in range(tm // K_TILE):
        vt_ref[0, j] = vt[:, j * K_TILE:(j + 1) * K_TILE]


def _qkv_call(x, pos3, inv_col, wt_qkv):
    B, S, _ = x.shape
    tm = QKV_ROW_TILE
    const = lambda b, i: (0, 0)
    return pl.pallas_call(
        _qkv_kernel,
        grid=(B, S // tm),
        in_specs=[pl.BlockSpec((1, tm, D_MODEL), lambda b, i: (b, i, 0)),
                  pl.BlockSpec((1, 1, tm), lambda b, i: (b, 0, i)),
                  pl.BlockSpec((HALF, 1), const),
                  pl.BlockSpec((3 * D_MODEL, D_MODEL), const)],
        out_specs=[pl.BlockSpec((1, tm // Q_TILE, D_MODEL, Q_TILE), lambda b, i: (b, i, 0, 0)),
                   pl.BlockSpec((1, tm, D_MODEL), lambda b, i: (b, i, 0)),
                   pl.BlockSpec((1, tm // K_TILE, D_MODEL, K_TILE), lambda b, i: (b, i, 0, 0))],
        out_shape=[jax.ShapeDtypeStruct((B, S // Q_TILE, D_MODEL, Q_TILE), jnp.bfloat16),
                   jax.ShapeDtypeStruct((B, S, D_MODEL), jnp.bfloat16),
                   jax.ShapeDtypeStruct((B, S // K_TILE, D_MODEL, K_TILE), jnp.bfloat16)],
        compiler_params=pltpu.CompilerParams(
            dimension_semantics=("arbitrary", "arbitrary"), vmem_limit_bytes=VMEM_LIMIT),
        name="qkv_rope",
    )(x, pos3, inv_col, wt_qkv)


def _attn_schedule(nq):
    tiles_per_q = Q_TILE // K_TILE
    qs, ks = [], []
    for qi in range(nq):
        for kj in range((qi + 1) * tiles_per_q):
            qs.append(qi)
            ks.append(kj)
    return np.asarray(qs, np.int32), np.asarray(ks, np.int32)


def _attn_kernel(qtab_ref, ktab_ref, lam_ref, g_ref, qt_ref, k_ref, vt_ref, o_ref,
                 s_ref, m_ref, l_ref, acc_ref, *, n_steps, unroll):
    nq = qt_ref.shape[1]
    n_groups = n_steps // unroll
    chunks_per_k = K_TILE // CHUNK
    chunks_per_q = Q_TILE // CHUNK
    lp = lam_ref[...]
    lam = (jnp.exp(jnp.sum(lp[0:1] * lp[1:2], axis=-1, keepdims=True))
           - jnp.exp(jnp.sum(lp[2:3] * lp[3:4], axis=-1, keepdims=True))
           + LAMBDA_INIT)
    qry_chunk = lax.broadcasted_iota(jnp.int32, (1, Q_TILE), 1) // CHUNK
    pen_row = lax.broadcasted_iota(jnp.int32, (PEN_ROWS, Q_TILE), 0)
    key_row_chunk = lax.broadcasted_iota(jnp.int32, (K_TILE, LANES), 0) // CHUNK
    key_lane = lax.broadcasted_iota(jnp.int32, (K_TILE, LANES), 1)
    chunk_onehot = jnp.where(key_row_chunk == key_lane, 1.0, 0.0).astype(jnp.bfloat16)
    ones_rows = jnp.ones((ONES_ROWS, K_TILE), jnp.bfloat16)
    zeros = jnp.zeros((LANES, Q_TILE), jnp.bfloat16)

    m_ref[...] = jnp.full_like(m_ref, NEG_BIG)
    l_ref[...] = jnp.zeros_like(l_ref)
    acc_ref[...] = jnp.zeros_like(acc_ref)

    def scores(g, slot):
        for u in range(unroll):
            qi = qtab_ref[g * unroll + u]
            kj = ktab_ref[g * unroll + u]
            qt = qt_ref[0, qi]
            k0 = pl.multiple_of(kj * K_TILE, K_TILE)
            kt = jnp.concatenate([k_ref[0, pl.ds(k0, K_TILE), :], chunk_onehot], axis=1)
            hidden = (qry_chunk + qi * chunks_per_q) < (kj * chunks_per_k + pen_row)
            pen = jnp.where(hidden & (pen_row < chunks_per_k), NEG_BIG, 0.0).astype(jnp.bfloat16)
            pen = jnp.concatenate([pen, zeros[:LANES - PEN_ROWS]], axis=0)
            w0 = jnp.concatenate([qt[:HEAD_DIM], zeros[:HEAD_DIM], pen], axis=0)
            w1 = jnp.concatenate([zeros[:HEAD_DIM], qt[HEAD_DIM:], pen], axis=0)
            s_ref[slot, u, 0] = jnp.dot(kt, w0, preferred_element_type=jnp.float32)
            s_ref[slot, u, 1] = jnp.dot(kt, w1, preferred_element_type=jnp.float32)

    def softmax_pv(g, slot):
        for u in range(unroll):
            qi = qtab_ref[g * unroll + u]
            kj = ktab_ref[g * unroll + u]
            vt1 = jnp.concatenate([vt_ref[0, kj], ones_rows], axis=0)
            for c in range(2):
                s = s_ref[slot, u, c]
                m = m_ref[qi, c]
                m_new = jnp.maximum(m, jnp.max(s, axis=0, keepdims=True))
                p = jnp.exp2(s - m_new)
                alpha = jnp.exp2(m - m_new)
                pv = jnp.dot(vt1, p.astype(jnp.bfloat16), preferred_element_type=jnp.float32)
                m_ref[qi, c] = m_new
                l_ref[qi, c] = alpha * l_ref[qi, c] + pv[V_DIM:V_DIM + 1]
                acc_ref[qi, c] = alpha * acc_ref[qi, c] + pv[:V_DIM]

    scores(0, 0)

    def group_pair(j, carry):
        scores(2 * j + 1, 1)
        softmax_pv(2 * j, 0)
        scores(2 * j + 2, 0)
        softmax_pv(2 * j + 1, 1)
        return carry

    lax.fori_loop(0, n_groups // 2 - 1, group_pair, 0)
    scores(n_groups - 1, 1)
    softmax_pv(n_groups - 2, 0)
    softmax_pv(n_groups - 1, 1)

    def finish(qi, carry):
        o = acc_ref[qi, 0] * (1.0 / l_ref[qi, 0]) - acc_ref[qi, 1] * (lam / l_ref[qi, 1])
        y = o * lax.rsqrt(jnp.mean(o * o, axis=0, keepdims=True) + LN_EPS)
        q0 = pl.multiple_of(qi * Q_TILE, Q_TILE)
        o_ref[0, pl.ds(q0, Q_TILE), :] = y.T * (g_ref[...] * (1.0 - LAMBDA_INIT))
        return carry

    lax.fori_loop(0, nq, finish, 0, unroll=True)


def _attn_call(lam_params, subln_g, qt, k, vt):
    B, S, _ = k.shape
    nq = S // Q_TILE
    qtab, ktab = _attn_schedule(nq)
    n_steps = len(qtab)
    unroll = max(u for u in range(1, KV_UNROLL + 1) if n_steps % (2 * u) == 0)
    const = lambda b, h, *_: (0, 0)
    grid_spec = pltpu.PrefetchScalarGridSpec(
        num_scalar_prefetch=2,
        grid=(B, N_HEADS),
        in_specs=[pl.BlockSpec((4, HEAD_DIM), const),
                  pl.BlockSpec((1, V_DIM), const),
                  pl.BlockSpec((1, nq, V_DIM, Q_TILE), lambda b, h, *_: (b, 0, h, 0)),
                  pl.BlockSpec((1, S, LANES), lambda b, h, *_: (b, 0, h)),
                  pl.BlockSpec((1, S // K_TILE, V_DIM, K_TILE), lambda b, h, *_: (b, 0, h, 0))],
        out_specs=pl.BlockSpec((1, S, LANES), lambda b, h, *_: (b, 0, h)),
        scratch_shapes=[pltpu.VMEM((2, unroll, 2, K_TILE, Q_TILE), jnp.float32),
                        pltpu.VMEM((nq, 2, 1, Q_TILE), jnp.float32),
                        pltpu.VMEM((nq, 2, 1, Q_TILE), jnp.float32),
                        pltpu.VMEM((nq, 2, V_DIM, Q_TILE), jnp.float32)])
    return pl.pallas_call(
        functools.partial(_attn_kernel, n_steps=n_steps, unroll=unroll),
        grid_spec=grid_spec,
        out_shape=jax.ShapeDtypeStruct((B, S, D_MODEL), jnp.float32),
        compiler_params=pltpu.CompilerParams(
            dimension_semantics=("arbitrary", "arbitrary"), vmem_limit_bytes=VMEM_LIMIT),
        name="diff_attn",
    )(jnp.asarray(qtab), jnp.asarray(ktab), lam_params, subln_g, qt, k, vt)


def _depthwise_conv(ext_ref, dwk_ref, dwb_ref, out_ref, ts):
    base = CONV_HALO - (CONV_WIDTH - 1)
    groups = [[(j, (base + j) // 8) for j in range(CONV_WIDTH) if (base + j) % 8 == r]
              for r in range(8)]
    max_a = max(a for g in groups for _, a in g)
    rows = CONV_ROWS + CONV_PAD
    assert rows + 8 * max_a <= CONV_ROWS + CONV_HALO + CONV_PAD

    def col_body(c, carry):
        lanes = pl.ds(pl.multiple_of(c * LANES, LANES), LANES)
        kcol = dwk_ref[:, lanes]
        bias = dwb_ref[:, lanes]
        for rb in range(ts // CONV_ROWS):
            t0 = rb * CONV_ROWS
            e = ext_ref[t0:t0 + rows + 8 * max_a, lanes]
            w = None
            for r in range(7, -1, -1):
                v = None
                for j, a in groups[r]:
                    term = e[8 * a:8 * a + rows] * kcol[j:j + 1]
                    v = term if v is None else v + term
                if w is not None:
                    v = v + pltpu.roll(w, rows - 1, 0)
                w = v
            out_ref[t0:t0 + CONV_ROWS, lanes] = w[:CONV_ROWS] + bias
        return carry

    lax.fori_loop(0, D_MODEL // LANES, col_body, 0)


def _conv_kernel(x_ref, wga_ref, wgb_ref, bg_ref, dwk_ref, dwb_ref, lng_ref, lnb_ref,
                 wp_ref, bp_ref, o_ref, ext_ref, acc_ref):
    ts = x_ref.shape[1]

    @pl.when(pl.program_id(1) == 0)
    def _():
        ext_ref[0:CONV_HALO, :] = jnp.zeros((CONV_HALO, D_MODEL), jnp.float32)

    half = ts // 2
    for h in range(2):
        xb = x_ref[0, h * half:(h + 1) * half, :].astype(jnp.bfloat16)
        ga = jnp.dot(xb, wga_ref[...], preferred_element_type=jnp.float32) + bg_ref[:, :D_MODEL]
        gb = jnp.dot(xb, wgb_ref[...], preferred_element_type=jnp.float32) + bg_ref[:, D_MODEL:]
        ext_ref[CONV_HALO + h * half:CONV_HALO + (h + 1) * half, :] = ga * jax.nn.sigmoid(gb)
    ext_ref[CONV_HALO + ts:, :] = jnp.zeros((CONV_PAD, D_MODEL), jnp.float32)
    _depthwise_conv(ext_ref, dwk_ref, dwb_ref, acc_ref, ts)
    ext_ref[0:CONV_HALO, :] = ext_ref[ts:ts + CONV_HALO, :]
    for h in range(2):
        rows = slice(h * half, (h + 1) * half)
        y = _layer_norm(acc_ref[rows, :], lng_ref[...], lnb_ref[...])
        y = y * jax.nn.sigmoid(y)
        o_ref[0, rows, :] = (jnp.dot(y.astype(jnp.bfloat16), wp_ref[...],
                                     preferred_element_type=jnp.float32) + bp_ref[...])


def _conv_call(x, w_in_bf16, b_glu, dw_kernel, dw_bias, ln_g, ln_b, w_pw2, b_pw2):
    B, S, _ = x.shape
    ts = CONV_ROW_TILE
    row = lambda b, i: (b, i, 0)
    const = lambda b, i: (0, 0)
    vec = pl.BlockSpec((1, D_MODEL), const)
    glu_col = GLU_COL0 // D_MODEL
    return pl.pallas_call(
        _conv_kernel,
        grid=(B, S // ts),
        in_specs=[pl.BlockSpec((1, ts, D_MODEL), row),
                  pl.BlockSpec((D_MODEL, D_MODEL), lambda b, i: (0, glu_col)),
                  pl.BlockSpec((D_MODEL, D_MODEL), lambda b, i: (0, glu_col + 1)),
                  pl.BlockSpec((1, 2 * D_MODEL), const),
                  pl.BlockSpec((CONV_WIDTH, D_MODEL), const),
                  vec, vec, vec,
                  pl.BlockSpec((D_MODEL, D_MODEL), const),
                  vec],
        out_specs=pl.BlockSpec((1, ts, D_MODEL), row),
        out_shape=jax.ShapeDtypeStruct((B, S, D_MODEL), jnp.float32),
        scratch_shapes=[pltpu.VMEM((ts + CONV_HALO + CONV_PAD, D_MODEL), jnp.float32),
                        pltpu.VMEM((ts, D_MODEL), jnp.float32)],
        compiler_params=pltpu.CompilerParams(
            dimension_semantics=("arbitrary", "arbitrary"), vmem_limit_bytes=VMEM_LIMIT),
        name="conv_branch",
    )(x, w_in_bf16, w_in_bf16, b_glu, dw_kernel, dw_bias, ln_g, ln_b, w_pw2, b_pw2)


def _merge_mlp_kernel(x_ref, att_ref, conv_ref, wga_ref, wgc_ref, bgate_ref, wout_ref,
                      l1g_ref, l1b_ref, w1_ref, w2_ref, l2g_ref, l2b_ref, o_ref):
    half_rows = ROW_TILE // 2

    def merge(rows):
        x = x_ref[rows, :]
        xb = x.astype(jnp.bfloat16)
        g_att = jax.nn.sigmoid(jnp.dot(xb, wga_ref[...], preferred_element_type=jnp.float32)
                               + bgate_ref[:, :D_MODEL])
        g_conv = jax.nn.sigmoid(jnp.dot(xb, wgc_ref[...], preferred_element_type=jnp.float32)
                                + bgate_ref[:, D_MODEL:])
        mix = g_att * att_ref[rows, :] + g_conv * conv_ref[rows, :]
        mixed = jnp.dot(mix.astype(jnp.bfloat16), wout_ref[...], preferred_element_type=jnp.float32)
        return _layer_norm(DEEPNORM_ALPHA * x + mixed, l1g_ref[...], l1b_ref[...])

    def mlp(h1):
        hb = h1.astype(jnp.bfloat16)
        ff = None
        for c in range(D_FF // D_MODEL):
            a = jnp.dot(hb, w1_ref[:, c * D_MODEL:(c + 1) * D_MODEL],
                        preferred_element_type=jnp.float32)
            a = jnp.maximum(a, 0.0)
            a = (a * a).astype(jnp.bfloat16)
            part = jnp.dot(a, w2_ref[c * D_MODEL:(c + 1) * D_MODEL, :],
                           preferred_element_type=jnp.float32)
            ff = part if ff is None else ff + part
        return _layer_norm(DEEPNORM_ALPHA * h1 + ff, l2g_ref[...], l2b_ref[...])

    halves = [slice(h * half_rows, (h + 1) * half_rows) for h in range(2)]
    h1s = [merge(rows) for rows in halves]
    for rows, h1 in zip(halves, h1s):
        o_ref[rows, :] = mlp(h1)


def _merge_mlp_call(x2, att2, conv2, w_in_bf16, b_gate, w_out, ln1_g, ln1_b, w_ff1, w_ff2, ln2_g, ln2_b):
    T = x2.shape[0]
    tm = ROW_TILE
    row = lambda i: (i, 0)
    const = lambda i: (0, 0)
    wspec = lambda shape, index_map=const: pl.BlockSpec(shape, index_map, pipeline_mode=pl.Buffered(1))
    tile = pl.BlockSpec((tm, D_MODEL), row)
    vec = wspec((1, D_MODEL))
    gate_col = GATE_COL0 // D_MODEL
    return pl.pallas_call(
        _merge_mlp_kernel,
        grid=(T // tm,),
        in_specs=[tile, tile, tile,
                  wspec((D_MODEL, D_MODEL), lambda i: (0, gate_col)),
                  wspec((D_MODEL, D_MODEL), lambda i: (0, gate_col + 1)),
                  wspec((1, 2 * D_MODEL)),
                  wspec((D_MODEL, D_MODEL)), vec, vec,
                  wspec((D_MODEL, D_FF)), wspec((D_FF, D_MODEL)), vec, vec],
        out_specs=tile,
        out_shape=jax.ShapeDtypeStruct((T, D_MODEL), jnp.float32),
        compiler_params=pltpu.CompilerParams(
            dimension_semantics=("arbitrary",), vmem_limit_bytes=VMEM_LIMIT),
        name="merge_mlp",
    )(x2, att2, conv2, w_in_bf16, w_in_bf16, b_gate, w_out, ln1_g, ln1_b, w_ff1, w_ff2, ln2_g, ln2_b)


def kernel(x, positions, w_in, b_glu, b_gate, lambda_q1, lambda_k1, lambda_q2, lambda_k2, subln_g, dw_kernel, dw_bias, conv_ln_g, conv_ln_b, w_pw2, b_pw2, w_out, ln1_g, ln1_b, w_ff1, w_ff2, ln2_g, ln2_b):
    B, S, D = x.shape
    T = B * S
    assert D == D_MODEL and w_in.shape == (DEPTH, D_MODEL, GATE_COL0 + 2 * D_MODEL)
    assert S % QKV_ROW_TILE == 0 and S % CONV_ROW_TILE == 0 and T % ROW_TILE == 0
    assert QKV_ROW_TILE % Q_TILE == 0 and QKV_ROW_TILE % K_TILE == 0 and Q_TILE % K_TILE == 0
    bf16 = jnp.bfloat16
    inv_freq = ROPE_THETA ** (-jnp.arange(HALF, dtype=jnp.float32) * 2.0 / HEAD_DIM)
    inv_col = inv_freq[:, None]
    pos3 = positions.astype(jnp.float32).reshape(B, 1, S)
    x2 = x.reshape(T, D)

    l = 0
    w = w_in[l].astype(bf16)
    wt_qkv = w[:, :GLU_COL0].T
    row = lambda a: a[l][None, :]

    qt, k, vt = _qkv_call(x, pos3, inv_col, wt_qkv)
    lam_params = jnp.stack([lambda_q1[l], lambda_k1[l], lambda_q2[l], lambda_k2[l]])
    att = _attn_call(lam_params, row(subln_g), qt, k, vt)
    conv = _conv_call(x, w, row(b_glu), dw_kernel[l], row(dw_bias),
                      row(conv_ln_g), row(conv_ln_b), w_pw2[l].astype(bf16), row(b_pw2))
    out = _merge_mlp_call(x2, att.reshape(T, D), conv.reshape(T, D), w, row(b_gate),
                          w_out[l].astype(bf16), row(ln1_g), row(ln1_b),
                          w_ff1[l].astype(bf16), w_ff2[l].astype(bf16), row(ln2_g), row(ln2_b))
    return out.reshape(B, S, D)
```

```python
import functools
import math

import jax
import jax.numpy as jnp
import numpy as np
from jax import lax
from jax.experimental import pallas as pl
from jax.experimental.pallas import tpu as pltpu

D_MODEL = 1024
N_HEADS = 8
HEAD_DIM = 64
HALF = HEAD_DIM // 2
V_DIM = 2 * HEAD_DIM
CHUNK = 64
CONV_WIDTH = 31
D_FF = 4 * D_MODEL
ROPE_THETA = 10000.0
LN_EPS = 1e-5
DEPTH = 1
DEEPNORM_ALPHA = (2.0 * DEPTH) ** 0.25
LAMBDA_INIT = 0.8 - 0.6 * math.exp(-0.3 * 0)
GLU_COL0 = 3 * D_MODEL
GATE_COL0 = 5 * D_MODEL

LANES = 128
VMEM_LIMIT = 56 * 1024 * 1024

ROW_TILE = 512
QKV_ROW_TILE = 1024
CONV_ROW_TILE = 1024
Q_TILE = 512
K_TILE = 512
KV_UNROLL = 3
ONES_ROWS = 16
PEN_ROWS = 16
CONV_HALO = 32
CONV_PAD = 8
CONV_ROWS = 32
NEG_BIG = -1e30
LOG2_E = math.log2(math.e)

_NT = (((1,), (1,)), ((), ()))


def _layer_norm(y, g, b):
    mu = jnp.mean(y, axis=-1, keepdims=True)
    d = y - mu
    var = jnp.mean(d * d, axis=-1, keepdims=True)
    return d * lax.rsqrt(var + LN_EPS) * g + b


def _rope_rows(t, cos, sin):
    pieces = []
    for g in range(D_MODEL // HEAD_DIM):
        t1 = t[g * HEAD_DIM:g * HEAD_DIM + HALF]
        t2 = t[g * HEAD_DIM + HALF:(g + 1) * HEAD_DIM]
        pieces.append(t1 * cos - t2 * sin)
        pieces.append(t1 * sin + t2 * cos)
    return jnp.concatenate(pieces, axis=0)


def _qkv_kernel(x_ref, pos_ref, inv_ref, wt_ref, qt_ref, k_ref, vt_ref):
    tm = x_ref.shape[1]
    xb = x_ref[0].astype(jnp.bfloat16)
    ang = inv_ref[...] * pos_ref[0]
    cos = jnp.cos(ang)
    sin = jnp.sin(ang)
    scale = HEAD_DIM ** -0.5 * LOG2_E

    qt = lax.dot_general(wt_ref[0:D_MODEL, :], xb, _NT,
                         preferred_element_type=jnp.float32)
    qt = _rope_rows(qt, cos * scale, sin * scale).astype(qt_ref.dtype)
    for j in range(tm // Q_TILE):
        qt_ref[0, j] = qt[:, j * Q_TILE:(j + 1) * Q_TILE]

    kt = lax.dot_general(wt_ref[D_MODEL:2 * D_MODEL, :], xb, _NT,
                         preferred_element_type=jnp.float32)
    k_ref[0] = _rope_rows(kt, cos, sin).T.astype(k_ref.dtype)

    vt = lax.dot_general(wt_ref[2 * D_MODEL:3 * D_MODEL, :], xb, _NT,
                         preferred_element_type=jnp.float32).astype(vt_ref.dtype)
    for j in range(tm // K_TILE):
        vt_ref[0, j] = vt[:, j * K_TILE:(j + 1) * K_TILE]


def _qkv_call(x, pos3, inv_col, wt_qkv):
    B, S, _ = x.shape
    tm = QKV_ROW_TILE
    const = lambda b, i: (0, 0)
    return pl.pallas_call(
        _qkv_kernel,
        grid=(B, S // tm),
        in_specs=[pl.BlockSpec((1, tm, D_MODEL), lambda b, i: (b, i, 0)),
                  pl.BlockSpec((1, 1, tm), lambda b, i: (b, 0, i)),
                  pl.BlockSpec((HALF, 1), const),
                  pl.BlockSpec((3 * D_MODEL, D_MODEL), const)],
        out_specs=[pl.BlockSpec((1, tm // Q_TILE, D_MODEL, Q_TILE), lambda b, i: (b, i, 0, 0)),
                   pl.BlockSpec((1, tm, D_MODEL), lambda b, i: (b, i, 0)),
                   pl.BlockSpec((1, tm // K_TILE, D_MODEL, K_TILE), lambda b, i: (b, i, 0, 0))],
        out_shape=[jax.ShapeDtypeStruct((B, S // Q_TILE, D_MODEL, Q_TILE), jnp.bfloat16),
                   jax.ShapeDtypeStruct((B, S, D_MODEL), jnp.bfloat16),
                   jax.ShapeDtypeStruct((B, S // K_TILE, D_MODEL, K_TILE), jnp.bfloat16)],
        compiler_params=pltpu.CompilerParams(
            dimension_semantics=("arbitrary", "arbitrary"), vmem_limit_bytes=VMEM_LIMIT),
        name="qkv_rope",
    )(x, pos3, inv_col, wt_qkv)


def _attn_schedule(nq):
    tiles_per_q = Q_TILE // K_TILE
    qs, ks = [], []
    for qi in range(nq):
        for kj in range((qi + 1) * tiles_per_q):
            qs.append(qi)
            ks.append(kj)
    return np.asarray(qs, np.int32), np.asarray(ks, np.int32)


def _attn_kernel(qtab_ref, ktab_ref, lam_ref, g_ref, qt_ref, k_ref, vt_ref, o_ref,
                 s_ref, m_ref, l_ref, acc_ref, *, n_steps, unroll):
    nq = qt_ref.shape[1]
    n_groups = n_steps // unroll
    chunks_per_k = K_TILE // CHUNK
    chunks_per_q = Q_TILE // CHUNK
    lp = lam_ref[...]
    lam = (jnp.exp(jnp.sum(lp[0:1] * lp[1:2], axis=-1, keepdims=True))
           - jnp.exp(jnp.sum(lp[2:3] * lp[3:4], axis=-1, keepdims=True))
           + LAMBDA_INIT)
    qry_chunk = lax.broadcasted_iota(jnp.int32, (1, Q_TILE), 1) // CHUNK
    pen_row = lax.broadcasted_iota(jnp.int32, (PEN_ROWS, Q_TILE), 0)
    key_row_chunk = lax.broadcasted_iota(jnp.int32, (K_TILE, LANES), 0) // CHUNK
    key_lane = lax.broadcasted_iota(jnp.int32, (K_TILE, LANES), 1)
    chunk_onehot = jnp.where(key_row_chunk == key_lane, 1.0, 0.0).astype(jnp.bfloat16)
    ones_rows = jnp.ones((ONES_ROWS, K_TILE), jnp.bfloat16)
    zeros = jnp.zeros((LANES, Q_TILE), jnp.bfloat16)

    m_ref[...] = jnp.full_like(m_ref, NEG_BIG)
    l_ref[...] = jnp.zeros_like(l_ref)
    acc_ref[...] = jnp.zeros_like(acc_ref)

    def scores(g, slot):
        for u in range(unroll):
            qi = qtab_ref[g * unroll + u]
            kj = ktab_ref[g * unroll + u]
            qt = qt_ref[0, qi]
            k0 = pl.multiple_of(kj * K_TILE, K_TILE)
            kt = jnp.concatenate([k_ref[0, pl.ds(k0, K_TILE), :], chunk_onehot], axis=1)
            hidden = (qry_chunk + qi * chunks_per_q) < (kj * chunks_per_k + pen_row)
            pen = jnp.where(hidden & (pen_row < chunks_per_k), NEG_BIG, 0.0).astype(jnp.bfloat16)
            pen = jnp.concatenate([pen, zeros[:LANES - PEN_ROWS]], axis=0)
            w0 = jnp.concatenate([qt[:HEAD_DIM], zeros[:HEAD_DIM], pen], axis=0)
            w1 = jnp.concatenate([zeros[:HEAD_DIM], qt[HEAD_DIM:], pen], axis=0)
            s_ref[slot, u, 0] = jnp.dot(kt, w0, preferred_element_type=jnp.float32)
            s_ref[slot, u, 1] = jnp.dot(kt, w1, preferred_element_type=jnp.float32)

    def softmax_pv(g, slot):
        for u in range(unroll):
            qi = qtab_ref[g * unroll + u]
            kj = ktab_ref[g * unroll + u]
            vt1 = jnp.concatenate([vt_ref[0, kj], ones_rows], axis=0)
            for c in range(2):
                s = s_ref[slot, u, c]
                m = m_ref[qi, c]
                m_new = jnp.maximum(m, jnp.max(s, axis=0, keepdims=True))
                p = jnp.exp2(s - m_new)
                alpha = jnp.exp2(m - m_new)
                pv = jnp.dot(vt1, p.astype(jnp.bfloat16), preferred_element_type=jnp.float32)
                m_ref[qi, c] = m_new
                l_ref[qi, c] = alpha * l_ref[qi, c] + pv[V_DIM:V_DIM + 1]
                acc_ref[qi, c] = alpha * acc_ref[qi, c] + pv[:V_DIM]

    scores(0, 0)

    def group_pair(j, carry):
        scores(2 * j + 1, 1)
        softmax_pv(2 * j, 0)
        scores(2 * j + 2, 0)
        softmax_pv(2 * j + 1, 1)
        return carry

    lax.fori_loop(0, n_groups // 2 - 1, group_pair, 0)
    scores(n_groups - 1, 1)
    softmax_pv(n_groups - 2, 0)
    softmax_pv(n_groups - 1, 1)

    def finish(qi, carry):
        o = acc_ref[qi, 0] * (1.0 / l_ref[qi, 0]) - acc_ref[qi, 1] * (lam / l_ref[qi, 1])
        y = o * lax.rsqrt(jnp.mean(o * o, axis=0, keepdims=True) + LN_EPS)
        q0 = pl.multiple_of(qi * Q_TILE, Q_TILE)
        o_ref[0, pl.ds(q0, Q_TILE), :] = y.T * (g_ref[...] * (1.0 - LAMBDA_INIT))
        return carry

    lax.fori_loop(0, nq, finish, 0, unroll=True)


def _attn_call(lam_params, subln_g, qt, k, vt):
    B, S, _ = k.shape
    nq = S // Q_TILE
    qtab, ktab = _attn_schedule(nq)
    n_steps = len(qtab)
    unroll = max(u for u in range(1, KV_UNROLL + 1) if n_steps % (2 * u) == 0)
    const = lambda b, h, *_: (0, 0)
    grid_spec = pltpu.PrefetchScalarGridSpec(
        num_scalar_prefetch=2,
        grid=(B, N_HEADS),
        in_specs=[pl.BlockSpec((4, HEAD_DIM), const),
                  pl.BlockSpec((1, V_DIM), const),
                  pl.BlockSpec((1, nq, V_DIM, Q_TILE), lambda b, h, *_: (b, 0, h, 0)),
                  pl.BlockSpec((1, S, LANES), lambda b, h, *_: (b, 0, h)),
                  pl.BlockSpec((1, S // K_TILE, V_DIM, K_TILE), lambda b, h, *_: (b, 0, h, 0))],
        out_specs=pl.BlockSpec((1, S, LANES), lambda b, h, *_: (b, 0, h)),
        scratch_shapes=[pltpu.VMEM((2, unroll, 2, K_TILE, Q_TILE), jnp.float32),
                        pltpu.VMEM((nq, 2, 1, Q_TILE), jnp.float32),
                        pltpu.VMEM((nq, 2, 1, Q_TILE), jnp.float32),
                        pltpu.VMEM((nq, 2, V_DIM, Q_TILE), jnp.float32)])
    return pl.pallas_call(
        functools.partial(_attn_kernel, n_steps=n_steps, unroll=unroll),
        grid_spec=grid_spec,
        out_shape=jax.ShapeDtypeStruct((B, S, D_MODEL), jnp.float32),
        compiler_params=pltpu.CompilerParams(
            dimension_semantics=("arbitrary", "arbitrary"), vmem_limit_bytes=VMEM_LIMIT),
        name="diff_attn",
    )(jnp.asarray(qtab), jnp.asarray(ktab), lam_params, subln_g, qt, k, vt)


def _depthwise_conv(ext_ref, dwk_ref, dwb_ref, out_ref, ts):
    base = CONV_HALO - (CONV_WIDTH - 1)
    groups = [[(j, (base + j) // 8) for j in range(CONV_WIDTH) if (base + j) % 8 == r]
              for r in range(8)]
    max_a = max(a for g in groups for _, a in g)
    rows = CONV_ROWS + CONV_PAD
    assert rows + 8 * max_a <= CONV_ROWS + CONV_HALO + CONV_PAD

    def col_body(c, carry):
        lanes = pl.ds(pl.multiple_of(c * LANES, LANES), LANES)
        kcol = dwk_ref[:, lanes]
        bias = dwb_ref[:, lanes]
        for rb in range(ts // CONV_ROWS):
            t0 = rb * CONV_ROWS
            e = ext_ref[t0:t0 + rows + 8 * max_a, lanes]
            w = None
            for r in range(7, -1, -1):
                v = None
                for j, a in groups[r]:
                    term = e[8 * a:8 * a + rows] * kcol[j:j + 1]
                    v = term if v is None else v + term
                if w is not None:
                    v = v + pltpu.roll(w, rows - 1, 0)
                w = v
            out_ref[t0:t0 + CONV_ROWS, lanes] = w[:CONV_ROWS] + bias
        return carry

    lax.fori_loop(0, D_MODEL // LANES, col_body, 0)


def _conv_kernel(x_ref, wga_ref, wgb_ref, bg_ref, dwk_ref, dwb_ref, o_ref, ext_ref):
    ts = x_ref.shape[1]

    @pl.when(pl.program_id(1) == 0)
    def _():
        ext_ref[0:CONV_HALO, :] = jnp.zeros((CONV_HALO, D_MODEL), jnp.float32)

    half = ts // 2
    for h in range(2):
        xb = x_ref[0, h * half:(h + 1) * half, :].astype(jnp.bfloat16)
        ga = jnp.dot(xb, wga_ref[...], preferred_element_type=jnp.float32) + bg_ref[:, :D_MODEL]
        gb = jnp.dot(xb, wgb_ref[...], preferred_element_type=jnp.float32) + bg_ref[:, D_MODEL:]
        ext_ref[CONV_HALO + h * half:CONV_HALO + (h + 1) * half, :] = ga * jax.nn.sigmoid(gb)
    ext_ref[CONV_HALO + ts:, :] = jnp.zeros((CONV_PAD, D_MODEL), jnp.float32)
    _depthwise_conv(ext_ref, dwk_ref, dwb_ref, o_ref.at[0], ts)
    ext_ref[0:CONV_HALO, :] = ext_ref[ts:ts + CONV_HALO, :]


def _conv_call(x, w_in_bf16, b_glu, dw_kernel, dw_bias):
    B, S, _ = x.shape
    ts = CONV_ROW_TILE
    row = lambda b, i: (b, i, 0)
    const = lambda b, i: (0, 0)
    vec = pl.BlockSpec((1, D_MODEL), const)
    glu_col = GLU_COL0 // D_MODEL
    return pl.pallas_call(
        _conv_kernel,
        grid=(B, S // ts),
        in_specs=[pl.BlockSpec((1, ts, D_MODEL), row),
                  pl.BlockSpec((D_MODEL, D_MODEL), lambda b, i: (0, glu_col)),
                  pl.BlockSpec((D_MODEL, D_MODEL), lambda b, i: (0, glu_col + 1)),
                  pl.BlockSpec((1, 2 * D_MODEL), const),
                  pl.BlockSpec((CONV_WIDTH, D_MODEL), const),
                  vec],
        out_specs=pl.BlockSpec((1, ts, D_MODEL), row),
        out_shape=jax.ShapeDtypeStruct((B, S, D_MODEL), jnp.float32),
        scratch_shapes=[pltpu.VMEM((ts + CONV_HALO + CONV_PAD, D_MODEL), jnp.float32)],
        compiler_params=pltpu.CompilerParams(
            dimension_semantics=("arbitrary", "arbitrary"), vmem_limit_bytes=VMEM_LIMIT),
        name="conv_branch",
    )(x, w_in_bf16, w_in_bf16, b_glu, dw_kernel, dw_bias)


def _merge_mlp_kernel(x_ref, att_ref, dw_ref, clg_ref, clb_ref, wp_ref, bp_ref,
                      wga_ref, wgc_ref, bgate_ref, wout_ref,
                      l1g_ref, l1b_ref, w1_ref, w2_ref, l2g_ref, l2b_ref, o_ref):
    half_rows = ROW_TILE // 2

    def merge(rows):
        y = _layer_norm(dw_ref[rows, :], clg_ref[...], clb_ref[...])
        y = y * jax.nn.sigmoid(y)
        conv = jnp.dot(y.astype(jnp.bfloat16), wp_ref[...],
                       preferred_element_type=jnp.float32) + bp_ref[...]
        x = x_ref[rows, :]
        xb = x.astype(jnp.bfloat16)
        g_att = jax.nn.sigmoid(jnp.dot(xb, wga_ref[...], preferred_element_type=jnp.float32)
                               + bgate_ref[:, :D_MODEL])
        g_conv = jax.nn.sigmoid(jnp.dot(xb, wgc_ref[...], preferred_element_type=jnp.float32)
                                + bgate_ref[:, D_MODEL:])
        mix = g_att * att_ref[rows, :] + g_conv * conv
        mixed = jnp.dot(mix.astype(jnp.bfloat16), wout_ref[...], preferred_element_type=jnp.float32)
        return _layer_norm(DEEPNORM_ALPHA * x + mixed, l1g_ref[...], l1b_ref[...])

    def mlp(h1):
        hb = h1.astype(jnp.bfloat16)
        ff = None
        for c in range(D_FF // D_MODEL):
            a = jnp.dot(hb, w1_ref[:, c * D_MODEL:(c + 1) * D_MODEL],
                        preferred_element_type=jnp.float32)
            a = jnp.maximum(a, 0.0)
            a = (a * a).astype(jnp.bfloat16)
            part = jnp.dot(a, w2_ref[c * D_MODEL:(c + 1) * D_MODEL, :],
                           preferred_element_type=jnp.float32)
            ff = part if ff is None else ff + part
        return _layer_norm(DEEPNORM_ALPHA * h1 + ff, l2g_ref[...], l2b_ref[...])

    halves = [slice(h * half_rows, (h + 1) * half_rows) for h in range(2)]
    h1s = [merge(rows) for rows in halves]
    for rows, h1 in zip(halves, h1s):
        o_ref[rows, :] = mlp(h1)


def _merge_mlp_call(x2, att2, dw2, cln_g, cln_b, w_pw2, b_pw2, w_in_bf16, b_gate, w_out,
                    ln1_g, ln1_b, w_ff1, w_ff2, ln2_g, ln2_b):
    T = x2.shape[0]
    tm = ROW_TILE
    row = lambda i: (i, 0)
    const = lambda i: (0, 0)
    wspec = lambda shape, index_map=const: pl.BlockSpec(shape, index_map, pipeline_mode=pl.Buffered(1))
    tile = pl.BlockSpec((tm, D_MODEL), row)
    vec = wspec((1, D_MODEL))
    gate_col = GATE_COL0 // D_MODEL
    return pl.pallas_call(
        _merge_mlp_kernel,
        grid=(T // tm,),
        in_specs=[tile, tile, tile,
                  vec, vec, wspec((D_MODEL, D_MODEL)), vec,
                  wspec((D_MODEL, D_MODEL), lambda i: (0, gate_col)),
                  wspec((D_MODEL, D_MODEL), lambda i: (0, gate_col + 1)),
                  wspec((1, 2 * D_MODEL)),
                  wspec((D_MODEL, D_MODEL)), vec, vec,
                  wspec((D_MODEL, D_FF)), wspec((D_FF, D_MODEL)), vec, vec],
        out_specs=tile,
        out_shape=jax.ShapeDtypeStruct((T, D_MODEL), jnp.float32),
        compiler_params=pltpu.CompilerParams(
            dimension_semantics=("arbitrary",), vmem_limit_bytes=VMEM_LIMIT),
        name="merge_mlp",
    )(x2, att2, dw2, cln_g, cln_b, w_pw2, b_pw2, w_in_bf16, w_in_bf16, b_gate, w_out,
      ln1_g, ln1_b, w_ff1, w_ff2, ln2_g, ln2_b)


def kernel(x, positions, w_in, b_glu, b_gate, lambda_q1, lambda_k1, lambda_q2, lambda_k2, subln_g, dw_kernel, dw_bias, conv_ln_g, conv_ln_b, w_pw2, b_pw2, w_out, ln1_g, ln1_b, w_ff1, w_ff2, ln2_g, ln2_b):
    B, S, D = x.shape
    T = B * S
    assert D == D_MODEL and w_in.shape == (DEPTH, D_MODEL, GATE_COL0 + 2 * D_MODEL)
    assert S % QKV_ROW_TILE == 0 and S % CONV_ROW_TILE == 0 and T % ROW_TILE == 0
    assert QKV_ROW_TILE % Q_TILE == 0 and QKV_ROW_TILE % K_TILE == 0 and Q_TILE % K_TILE == 0
    bf16 = jnp.bfloat16
    inv_freq = ROPE_THETA ** (-jnp.arange(HALF, dtype=jnp.float32) * 2.0 / HEAD_DIM)
    inv_col = inv_freq[:, None]
    pos3 = positions.astype(jnp.float32).reshape(B, 1, S)
    x2 = x.reshape(T, D)

    l = 0
    w = w_in[l].astype(bf16)
    wt_qkv = w[:, :GLU_COL0].T
    row = lambda a: a[l][None, :]

    qt, k, vt = _qkv_call(x, pos3, inv_col, wt_qkv)
    lam_params = jnp.stack([lambda_q1[l], lambda_k1[l], lambda_q2[l], lambda_k2[l]])
    att = _attn_call(lam_params, row(subln_g), qt, k, vt)
    dw = _conv_call(x, w, row(b_glu), dw_kernel[l], row(dw_bias))
    out = _merge_mlp_call(x2, att.reshape(T, D), dw.reshape(T, D), row(conv_ln_g), row(conv_ln_b),
                          w_pw2[l].astype(bf16), row(b_pw2), w, row(b_gate),
                          w_out[l].astype(bf16), row(ln1_g), row(ln1_b),
                          w_ff1[l].astype(bf16), w_ff2[l].astype(bf16), row(ln2_g), row(ln2_b))
    return out.reshape(B, S, D)
```
